```python
import math
import jax
import jax.numpy as jnp
from jax import lax
import numpy as np

D_MODEL = 4096
BATCH = 1
SEQ = 8192
DEPTH = 2

CTX_LEN = 256
GRID_W = 64
EPS = 1e-6
MIX_WIDTH = D_MODEL
MOD_CHUNKS = 6

HG_WIDTH = MIX_WIDTH // 2
HG_HEAD_DIM = 128
HG_HEADS = HG_WIDTH // HG_HEAD_DIM
HG_CHUNK = 64
HG_COLS = 5 * HG_WIDTH

SSD_WIDTH = MIX_WIDTH - HG_WIDTH
SSD_HEAD_DIM = 64
SSD_HEADS = SSD_WIDTH // SSD_HEAD_DIM
SSD_GROUPS = 8
SSD_STATE = 128
SSD_CONV_WIDTH = 3
SSD_CHUNK = 128
SSD_BC = SSD_GROUPS * SSD_STATE
SSD_CONV_DIM = SSD_WIDTH + 2 * SSD_BC
SSD_COLS = SSD_WIDTH + SSD_CONV_DIM + 2 * SSD_HEADS
SSD_NORM_GROUP = SSD_WIDTH // SSD_GROUPS

IN_COLS = HG_COLS + SSD_COLS

MOE_GROUPS = 4
EXPERTS_PER_GROUP = 8
N_EXPERTS = MOE_GROUPS * EXPERTS_PER_GROUP
TOP_K = 2
EXPERT_HIDDEN = D_MODEL // 8
MOE_BLOCK = 128

kernel_name = 'hybrid_hgrn2_ssd_hmoe_prefix_dit'


def rmsnorm(u, w):
    uf = u.astype(jnp.float32)
    uf = uf * lax.rsqrt(jnp.mean(uf * uf, axis=-1, keepdims=True) + EPS)
    return (uf * w.astype(jnp.float32)).astype(u.dtype)


def group_rms(u, group):
    shp = u.shape
    g = u.reshape(shp[:-1] + (shp[-1] // group, group))
    g = g * lax.rsqrt(jnp.mean(g * g, axis=-1, keepdims=True) + EPS)
    return g.reshape(shp)


def flip(u):
    return jnp.flip(u, axis=1)


def to_col_major(u, rows):
    b, t, f = u.shape
    return u.reshape(b, rows, GRID_W, f).transpose(0, 2, 1, 3).reshape(b, t, f)


def to_row_major(u, rows):
    b, t, f = u.shape
    return u.reshape(b, GRID_W, rows, f).transpose(0, 2, 1, 3).reshape(b, t, f)


def dwconv_centred(u, w, bias):
    pad = w.shape[0] // 2
    y = lax.conv_general_dilated(u, w[:, None, :].astype(u.dtype), window_strides=(1,), padding=[(pad, pad)],
                                 dimension_numbers=('NWC', 'WIO', 'NWC'), feature_group_count=u.shape[-1])
    return y + bias.astype(u.dtype)


def gla_chunk_scan(q, k, v, logf, s0):
    bsz, t, h, _ = q.shape
    dv = v.shape[-1]
    n = t // HG_CHUNK

    def chunks(u):
        return u.reshape(bsz, n, HG_CHUNK, h, u.shape[-1]).transpose(1, 0, 3, 2, 4)

    tri = jnp.tril(jnp.ones((HG_CHUNK, HG_CHUNK), dtype=bool))

    def step(state, inp):
        qi, ki, vi, gi = inp
        b = jnp.cumsum(gi, axis=2)
        rel = b[:, :, :, None, :] - b[:, :, None, :, :]
        decay = jnp.exp(jnp.where(tri[:, :, None], rel, -jnp.inf))
        scores = jnp.einsum('bhtk,bhsk,bhtsk->bhts', qi, ki, decay)
        out = jnp.einsum('bhts,bhsv->bhtv', scores, vi) + jnp.einsum('bhtk,bhkv->bhtv', qi * jnp.exp(b), state)
        b_end = b[:, :, -1]
        state = state * jnp.exp(b_end)[..., None] + jnp.einsum('bhsk,bhsv->bhkv', ki * jnp.exp(b_end[:, :, None] - b), vi)
        return state, out

    s_fin, out = lax.scan(step, s0, (chunks(q), chunks(k), chunks(v), chunks(logf)))
    return out.transpose(1, 0, 3, 2, 4).reshape(bsz, t, h, dv), s_fin


def ssd_chunk_scan(xs, dt, a, bm, cm, s0):
    bsz, t, h, p = xs.shape
    g, ds = bm.shape[2], bm.shape[3]
    r = h // g
    n = t // SSD_CHUNK
    xc = xs.reshape(bsz, n, SSD_CHUNK, g, r, p)
    dtc = dt.reshape(bsz, n, SSD_CHUNK, g, r)
    bc = bm.reshape(bsz, n, SSD_CHUNK, g, ds)
    cc = cm.reshape(bsz, n, SSD_CHUNK, g, ds)
    cum = jnp.cumsum(dtc * a.reshape(g, r), axis=2)
    tri = jnp.tril(jnp.ones((SSD_CHUNK, SSD_CHUNK), dtype=bool))
    seg = cum[:, :, :, None] - cum[:, :, None]
    decay = jnp.exp(jnp.where(tri[:, :, None, None], seg, -jnp.inf))
    cb = jnp.einsum('bctgd,bcsgd->bctsg', cc, bc)
    xdt = xc * dtc[..., None]
    y_diag = jnp.einsum('bctsg,bctsgr,bcsgrp->bctgrp', cb, decay, xdt)
    decay_to_end = jnp.exp(cum[:, :, -1:] - cum)
    states = jnp.einsum('bcsgd,bcsgr,bcsgrp->bcgrpd', bc, decay_to_end, xdt)
    chunk_decay = jnp.exp(cum[:, :, -1])

    def step(s, inp):
        st, dec = inp
        return s * dec[..., None, None] + st, s

    s_fin, s_in = lax.scan(step, s0.reshape(bsz, g, r, p, ds), (jnp.moveaxis(states, 1, 0), jnp.moveaxis(chunk_decay, 1, 0)))
    s_in = jnp.moveaxis(s_in, 0, 1)
    y_off = jnp.einsum('bctgd,bcgrpd,bctgr->bctgrp', cc, s_in, jnp.exp(cum))
    return (y_diag + y_off).reshape(bsz, t, h, p), s_fin.reshape(bsz, h, p, ds)


def hgrn2_forget(pre, lb):
    s = pre.astype(jnp.float32)
    log_f = jnp.logaddexp(jnp.log(lb), jnp.log1p(-lb) + jax.nn.log_sigmoid(s))
    k = (1.0 - lb) * jax.nn.sigmoid(-s)
    return log_f, k


def hgrn2_mixer(p_x, p_c, lb, norm_w):
    def prep(p):
        b, t, _ = p.shape
        q, f_fw, f_bw, i, g = jnp.split(p, 5, axis=-1)
        heads = lambda u: u.astype(jnp.float32).reshape(b, t, HG_HEADS, HG_HEAD_DIM)
        lf_fw, k_fw = hgrn2_forget(f_fw, lb[0])
        lf_bw, k_bw = hgrn2_forget(f_bw, lb[1])
        return heads(q), heads(i), heads(lf_fw), heads(k_fw), heads(lf_bw), heads(k_bw), g

    qc, ic, lfc, kfc, lbc, kbc, gc = prep(p_c)
    qx, ix, lfx, kfx, lbx, kbx, gx = prep(p_x)
    s0 = jnp.zeros((p_x.shape[0], HG_HEADS, HG_HEAD_DIM, HG_HEAD_DIM), jnp.float32)
    oc_fw, sc_fw = gla_chunk_scan(qc, kfc, ic, lfc, s0)
    oc_bw, sc_bw = gla_chunk_scan(flip(qc), flip(kbc), flip(ic), flip(lbc), s0)
    ox_fw, _ = gla_chunk_scan(qx, kfx, ix, lfx, sc_fw)
    ox_bw, _ = gla_chunk_scan(flip(qx), flip(kbx), flip(ix), flip(lbx), sc_bw)

    def readout(o, g):
        b, t = o.shape[:2]
        o = o * lax.rsqrt(jnp.mean(o * o, axis=-1, keepdims=True) + EPS) * norm_w.astype(jnp.float32)
        return (o.reshape(b, t, HG_WIDTH) * jax.nn.silu(g.astype(jnp.float32))).astype(g.dtype)

    return readout(ox_fw + flip(ox_bw), gx), readout(oc_fw + flip(oc_bw), gc)


def ssd_mixer(p_x, p_c, conv_w, conv_b, dt_bias, a_log, d_skip, norm_w, rows):
    a = -jnp.exp(a_log.astype(jnp.float32))

    def prep(p, col_major):
        b, t, _ = p.shape
        z = p[..., :SSD_WIDTH]
        rest = p[..., SSD_WIDTH:]
        if col_major:
            rest = to_col_major(rest, rows)
        xbc = jax.nn.silu(dwconv_centred(rest[..., :SSD_CONV_DIM], conv_w, conv_b)).astype(jnp.float32)
        xs = xbc[..., :SSD_WIDTH].reshape(b, t, SSD_HEADS, SSD_HEAD_DIM)
        bm = xbc[..., SSD_WIDTH:SSD_WIDTH + SSD_BC].reshape(b, t, SSD_GROUPS, SSD_STATE)
        cm = xbc[..., SSD_WIDTH + SSD_BC:].reshape(b, t, SSD_GROUPS, SSD_STATE)
        dt = jax.nn.softplus(rest[..., SSD_CONV_DIM:].astype(jnp.float32).reshape(b, t, 2, SSD_HEADS) + dt_bias.astype(jnp.float32))
        return z, xs, bm, cm, dt

    zc, xc, bc, cc, dtc = prep(p_c, False)
    zx, xx, bx, cx, dtx = prep(p_x, True)
    s0 = jnp.zeros((p_x.shape[0], SSD_HEADS, SSD_HEAD_DIM, SSD_STATE), jnp.float32)
    yc_fw, sc_fw = ssd_chunk_scan(xc, dtc[:, :, 0], a[0], bc, cc, s0)
    yc_bw, sc_bw = ssd_chunk_scan(flip(xc), flip(dtc[:, :, 1]), a[1], flip(bc), flip(cc), s0)
    yx_fw, _ = ssd_chunk_scan(xx, dtx[:, :, 0], a[0], bx, cx, sc_fw)
    yx_bw, _ = ssd_chunk_scan(flip(xx), flip(dtx[:, :, 1]), a[1], flip(bx), flip(cx), sc_bw)
    d = d_skip.astype(jnp.float32)[:, None]

    def readout(y, z):
        b, t = y.shape[:2]
        y = y.reshape(b, t, SSD_WIDTH) * jax.nn.silu(z.astype(jnp.float32))
        return (group_rms(y, SSD_NORM_GROUP) * norm_w.astype(jnp.float32)).astype(z.dtype)

    y_x = to_row_major((yx_fw + flip(yx_bw) + d * xx).reshape(xx.shape[0], xx.shape[1], SSD_WIDTH), rows)
    y_c = yc_fw + flip(yc_bw) + d * xc
    return readout(y_x, zx), readout(y_c, zc)


def hier_moe(h, w_gr, b_gr, w_er, b_er, w_gate, w_up, w_down):
    n_tok, d = h.shape
    hf = h.astype(jnp.float32)
    g_prob = jax.nn.softmax(hf @ w_gr.astype(jnp.float32) + b_gr.astype(jnp.float32), axis=-1)
    g_top_p, g_top_i = lax.top_k(g_prob, 1)
    e_logits = (hf @ w_er.astype(jnp.float32) + b_er.astype(jnp.float32)).reshape(n_tok, MOE_GROUPS, EXPERTS_PER_GROUP)
    e_in_group = jnp.take_along_axis(e_logits, g_top_i[:, :, None], axis=1)[:, 0]
    e_top_v, e_top_i = lax.top_k(e_in_group, TOP_K)
    e_w = jax.nn.softmax(e_top_v, axis=-1) * g_top_p
    expert_id = g_top_i * EXPERTS_PER_GROUP + e_top_i

    n_assign = n_tok * TOP_K
    flat_e = expert_id.reshape(-1)
    flat_tok = jnp.repeat(jnp.arange(n_tok, dtype=jnp.int32), TOP_K)
    flat_w = e_w.reshape(-1)
    order = jnp.argsort(flat_e)
    se, st, sw = flat_e[order], flat_tok[order], flat_w[order]
    counts = jnp.bincount(flat_e, length=N_EXPERTS)
    padded = (counts + MOE_BLOCK - 1) // MOE_BLOCK * MOE_BLOCK
    pad_end = jnp.cumsum(padded)
    pad_start = pad_end - padded
    start = jnp.cumsum(counts) - counts
    dest = pad_start[se] + jnp.arange(n_assign, dtype=jnp.int32) - start[se]
    n_blocks = -(-n_assign // MOE_BLOCK) + N_EXPERTS
    buf_tok = jnp.full((n_blocks * MOE_BLOCK,), n_tok, jnp.int32).at[dest].set(st)
    buf_w = jnp.zeros((n_blocks * MOE_BLOCK,), jnp.float32).at[dest].set(sw)
    blk_e = jnp.minimum(jnp.searchsorted(pad_end, jnp.arange(n_blocks, dtype=jnp.int32) * MOE_BLOCK, side='right'), N_EXPERTS - 1)
    h_pad = jnp.concatenate([h, jnp.zeros((1, d), h.dtype)], axis=0)

    def expert_block(args):
        idx, e = args
        xb = h_pad[idx]
        return (jax.nn.silu(xb @ w_gate[e]) * (xb @ w_up[e])) @ w_down[e]

    yb = lax.map(expert_block, (buf_tok.reshape(n_blocks, MOE_BLOCK), blk_e))
    y = jax.ops.segment_sum(yb.reshape(-1, d) * buf_w[:, None].astype(yb.dtype), buf_tok, num_segments=n_tok + 1)
    return y[:n_tok]


def setup_inputs(seed: int = 0) -> dict:
    key = jax.random.key(seed)
    ks = jax.random.split(key, 26)
    f32 = jnp.float32

    def nrm(k, shape, scale):
        return jax.random.normal(k, shape, f32) * scale

    def gain(k, shape):
        return 1.0 + 0.02 * jax.random.normal(k, shape, f32)

    dt0 = jnp.exp(jax.random.uniform(ks[12], (DEPTH, 2, SSD_HEADS), f32, math.log(1e-3), math.log(1e-1)))
    return {
        'x': nrm(ks[0], (BATCH, SEQ, D_MODEL), 1.0),
        'c': nrm(ks[1], (BATCH, D_MODEL), 1.0),
        'ctx': nrm(ks[2], (BATCH, CTX_LEN, D_MODEL), 1.0),
        'c_ctx': nrm(ks[3], (D_MODEL,), 1.0),
        'w_mod': nrm(ks[4], (DEPTH, D_MODEL, MOD_CHUNKS * D_MODEL), 0.5 * D_MODEL ** -0.5),
        'b_mod': nrm(ks[5], (DEPTH, MOD_CHUNKS * D_MODEL), 0.02),
        'norm1_w': gain(ks[6], (DEPTH, D_MODEL)),
        'w_in': nrm(ks[7], (DEPTH, D_MODEL, IN_COLS), D_MODEL ** -0.5),
        'hgrn_lower_bounds': nrm(ks[8], (DEPTH, 2, HG_WIDTH), 0.5),
        'hgrn_norm_w': gain(ks[9], (DEPTH, HG_HEAD_DIM)),
        'ssd_conv_w': nrm(ks[10], (DEPTH, SSD_CONV_WIDTH, SSD_CONV_DIM), SSD_CONV_WIDTH ** -0.5),
        'ssd_conv_b': nrm(ks[11], (DEPTH, SSD_CONV_DIM), 0.02),
        'ssd_dt_bias': dt0 + jnp.log(-jnp.expm1(-dt0)),
        'ssd_a_log': jnp.log(jax.random.uniform(ks[13], (DEPTH, 2, SSD_HEADS), f32, 1.0, 16.0)),
        'ssd_d': gain(ks[14], (DEPTH, SSD_HEADS)),
        'ssd_norm_w': gain(ks[15], (DEPTH, SSD_WIDTH)),
        'w_out': nrm(ks[16], (DEPTH, MIX_WIDTH, D_MODEL), MIX_WIDTH ** -0.5),
        'norm2_w': gain(ks[17], (DEPTH, D_MODEL)),
        'w_group_router': nrm(ks[18], (DEPTH, D_MODEL, MOE_GROUPS), D_MODEL ** -0.5),
        'b_group_router': nrm(ks[19], (DEPTH, MOE_GROUPS), 0.01),
        'w_expert_router': nrm(ks[20], (DEPTH, D_MODEL, N_EXPERTS), D_MODEL ** -0.5),
        'b_expert_router': nrm(ks[21], (DEPTH, N_EXPERTS), 0.01),
        'w_gate': nrm(ks[22], (DEPTH, N_EXPERTS, D_MODEL, EXPERT_HIDDEN), D_MODEL ** -0.5),
        'w_up': nrm(ks[23], (DEPTH, N_EXPERTS, D_MODEL, EXPERT_HIDDEN), D_MODEL ** -0.5),
        'w_down': nrm(ks[24], (DEPTH, N_EXPERTS, EXPERT_HIDDEN, D_MODEL), EXPERT_HIDDEN ** -0.5),
        'final_norm_w': gain(ks[25], (D_MODEL,)),
    }


def reference(x, c, ctx, c_ctx, w_mod, b_mod, norm1_w, w_in, hgrn_lower_bounds, hgrn_norm_w, ssd_conv_w, ssd_conv_b,
              ssd_dt_bias, ssd_a_log, ssd_d, ssd_norm_w, w_out, norm2_w, w_group_router, b_group_router,
              w_expert_router, b_expert_router, w_gate, w_up, w_down, final_norm_w):
    rows = x.shape[1] // GRID_W
    n_lat = x.shape[0] * x.shape[1]
    lb_soft = jax.nn.softmax(hgrn_lower_bounds.astype(jnp.float32), axis=0)
    lower_bounds = jnp.cumsum(lb_soft, axis=0) - lb_soft[0]
    silu_c = jax.nn.silu(c)
    silu_cc = jax.nn.silu(c_ctx)
    for l in range(DEPTH):
        last = l == DEPTH - 1
        mod_x = jnp.split((silu_c @ w_mod[l] + b_mod[l])[:, None, :], MOD_CHUNKS, axis=-1)
        mod_c = jnp.split(silu_cc @ w_mod[l] + b_mod[l], MOD_CHUNKS, axis=-1)

        h_x = rmsnorm(x, norm1_w[l]) * (1 + mod_x[1]) + mod_x[0]
        h_c = rmsnorm(ctx, norm1_w[l]) * (1 + mod_c[1]) + mod_c[0]
        p_x = h_x @ w_in[l]
        p_c = h_c @ w_in[l]
        a_x, a_c = hgrn2_mixer(p_x[..., :HG_COLS], p_c[..., :HG_COLS], lower_bounds[l], hgrn_norm_w[l])
        b_x, b_c = ssd_mixer(p_x[..., HG_COLS:], p_c[..., HG_COLS:], ssd_conv_w[l], ssd_conv_b[l], ssd_dt_bias[l],
                             ssd_a_log[l], ssd_d[l], ssd_norm_w[l], rows)
        x = x + mod_x[2] * (jnp.concatenate([a_x, b_x], axis=-1) @ w_out[l])
        if not last:
            ctx = ctx + mod_c[2] * (jnp.concatenate([a_c, b_c], axis=-1) @ w_out[l])

        moe_p = (w_group_router[l], b_group_router[l], w_expert_router[l], b_expert_router[l], w_gate[l], w_up[l], w_down[l])
        f_x = rmsnorm(x, norm2_w[l]) * (1 + mod_x[4]) + mod_x[3]
        if last:
            y = hier_moe(f_x.reshape(n_lat, D_MODEL), *moe_p)
            x = x + mod_x[5] * y.reshape(x.shape)
        else:
            f_c = rmsnorm(ctx, norm2_w[l]) * (1 + mod_c[4]) + mod_c[3]
            y = hier_moe(jnp.concatenate([f_x.reshape(n_lat, D_MODEL), f_c.reshape(-1, D_MODEL)], axis=0), *moe_p)
            x = x + mod_x[5] * y[:n_lat].reshape(x.shape)
            ctx = ctx + mod_c[5] * y[n_lat:].reshape(ctx.shape)
    return rmsnorm(x, final_norm_w)
```

```python
import functools

import numpy as np
import jax
import jax.numpy as jnp
from jax import lax
from jax.experimental import pallas as pl
from jax.experimental.pallas import tpu as pltpu

F32 = jnp.float32
BF16 = jnp.bfloat16
I32 = jnp.int32

EPS = 1e-6
GRID_W = 64
MOD_CHUNKS = 6

HG_HEAD_DIM = 128
SSD_HEAD_DIM = 64
SSD_GROUPS = 8
SSD_STATE = 128
MOE_GROUPS = 4
EXPERTS_PER_GROUP = 8
N_EXPERTS = MOE_GROUPS * EXPERTS_PER_GROUP
TOP_K = 2

LANES = 128
SUBLANES = 8
VMEM_BYTES_V7X = 64 * 1024 * 1024
HG_CHUNK = 64
HG_SUB = 16
HG_HEADS_PER_STEP = 4
SSD_CHUNK = 128
MOE_ROWS = 256


def _vmem_limit(nbytes):
    return int(min(max(nbytes * 3 // 2, 16 * 1024 * 1024), VMEM_BYTES_V7X - 8 * 1024 * 1024))


def _split3(x):
    h = x.astype(BF16)
    r = x - h.astype(F32)
    m = r.astype(BF16)
    lo = (r - m.astype(F32)).astype(BF16)
    return h, m, lo


def _dot3(a01, x, dims=None):
    out = None
    for part in _split3(x):
        if dims is None:
            t = jnp.dot(a01, part, preferred_element_type=F32)
        else:
            t = lax.dot_general(a01, part, dims, preferred_element_type=F32)
        out = t if out is None else out + t
    return out


def _dot3_rhs01(x, b01):
    out = None
    for part in _split3(x):
        t = jnp.dot(part, b01, preferred_element_type=F32)
        out = t if out is None else out + t
    return out


def _silu(x):
    return x * jax.nn.sigmoid(x)


def _mod_kernel(c_ref, w_ref, b_ref, o_ref, acc_ref):
    k = pl.program_id(1)

    @pl.when(k == 0)
    def _():
        acc_ref[...] = jnp.zeros_like(acc_ref)

    c = c_ref[...]
    s = _silu(c)
    w = w_ref[...]
    tk, tn = w.shape
    for r in range(2):
        prod = w * s[:, r:r + 1]
        acc_ref[r] += jnp.sum(prod.reshape(tk // SUBLANES, SUBLANES, tn), axis=0)

    @pl.when(k == pl.num_programs(1) - 1)
    def _():
        o_ref[...] = jnp.sum(acc_ref[...], axis=1) + b_ref[...]


def _mod_vectors(c2t, w, b):
    d, n = w.shape
    tk, tn = min(1024, d), min(2048, n)
    return pl.pallas_call(
        _mod_kernel,
        out_shape=jax.ShapeDtypeStruct((2, n), F32),
        grid=(n // tn, d // tk),
        in_specs=[pl.BlockSpec((tk, 2), lambda j, k: (k, 0)),
                  pl.BlockSpec((tk, tn), lambda j, k: (k, j)),
                  pl.BlockSpec((1, tn), lambda j, k: (0, j))],
        out_specs=pl.BlockSpec((2, tn), lambda j, k: (0, j)),
        scratch_shapes=[pltpu.VMEM((2, SUBLANES, tn), F32)],
        compiler_params=pltpu.CompilerParams(
            dimension_semantics=("arbitrary", "arbitrary"),
            vmem_limit_bytes=_vmem_limit(2 * tk * tn * 4 + 4 * tk * tn)),
        name="adaln_matvec",
    )(c2t, w, b)


def _row_select(i, tm, n_x, ref):
    row = i * tm + lax.broadcasted_iota(I32, (tm, 1), 0)
    return jnp.where(row < n_x, ref[0:1, :], ref[1:2, :])


def _norm_mod_kernel(x_ref, w_ref, sh_ref, sc_ref, o_ref, *, n_x, tm):
    i = pl.program_id(0)
    x = x_ref[...]
    ms = jnp.mean(x * x, axis=-1, keepdims=True)
    y = x * lax.rsqrt(ms + EPS) * w_ref[...]
    sc = _row_select(i, tm, n_x, sc_ref)
    sh = _row_select(i, tm, n_x, sh_ref)
    o_ref[...] = (y * (1.0 + sc) + sh).astype(o_ref.dtype)


def _norm_mod(x, w, shift2, scale2, n_rows, n_x, out_dtype):
    d = x.shape[1]
    tm = 256
    return pl.pallas_call(
        functools.partial(_norm_mod_kernel, n_x=n_x, tm=tm),
        out_shape=jax.ShapeDtypeStruct((n_rows, d), out_dtype),
        grid=(n_rows // tm,),
        in_specs=[pl.BlockSpec((tm, d), lambda i: (i, 0)),
                  pl.BlockSpec((1, d), lambda i: (0, 0)),
                  pl.BlockSpec((2, d), lambda i: (0, 0)),
                  pl.BlockSpec((2, d), lambda i: (0, 0))],
        out_specs=pl.BlockSpec((tm, d), lambda i: (i, 0)),
        compiler_params=pltpu.CompilerParams(
            dimension_semantics=("arbitrary",),
            vmem_limit_bytes=_vmem_limit(4 * tm * d * 4 + 4 * tm * d * 4)),
        name="norm_modulate",
    )(x, w, shift2, scale2)


def _mm_kernel(a_ref, w_ref, o_ref, wb_ref):
    @pl.when(pl.program_id(1) == 0)
    def _():
        wb_ref[...] = w_ref[...].astype(BF16)

    o_ref[...] = jnp.dot(a_ref[...], wb_ref[...], preferred_element_type=F32)


def _mm_res_kernel(a_ref, w_ref, r_ref, g_ref, o_ref, wb_ref, *, n_x, tm):
    i = pl.program_id(1)

    @pl.when(i == 0)
    def _():
        wb_ref[...] = w_ref[...].astype(BF16)

    acc = jnp.dot(a_ref[...], wb_ref[...], preferred_element_type=F32)
    o_ref[...] = r_ref[...] + _row_select(i, tm, n_x, g_ref) * acc


def _mm_tiles(m, k, n):
    tm = 1056 if m % 1056 == 0 else (1024 if m % 1024 == 0 else m)
    tn = 512 if n % 512 == 0 else n
    need = 2 * k * tn * 4 + k * tn * 2 + 2 * tm * k * 2 + 4 * tm * tn * 4
    return tm, tn, need


def _matmul(a, w, col_off, n):
    m, k = a.shape
    tm, tn, need = _mm_tiles(m, k, n)
    off = col_off // tn
    assert off * tn == col_off
    return pl.pallas_call(
        _mm_kernel,
        out_shape=jax.ShapeDtypeStruct((m, n), F32),
        grid=(n // tn, m // tm),
        in_specs=[pl.BlockSpec((tm, k), lambda j, i: (i, 0)),
                  pl.BlockSpec((k, tn), lambda j, i: (0, j + off))],
        out_specs=pl.BlockSpec((tm, tn), lambda j, i: (i, j)),
        scratch_shapes=[pltpu.VMEM((k, tn), BF16)],
        compiler_params=pltpu.CompilerParams(
            dimension_semantics=("arbitrary", "arbitrary"), vmem_limit_bytes=_vmem_limit(need)),
        name="proj_matmul",
    )(a, w)


def _matmul_residual(a, w, res, gate2, n_x):
    m, k = a.shape
    n = w.shape[1]
    tm, tn, need = _mm_tiles(m, k, n)
    return pl.pallas_call(
        functools.partial(_mm_res_kernel, n_x=n_x, tm=tm),
        out_shape=jax.ShapeDtypeStruct((m, n), F32),
        grid=(n // tn, m // tm),
        in_specs=[pl.BlockSpec((tm, k), lambda j, i: (i, 0)),
                  pl.BlockSpec((k, tn), lambda j, i: (0, j)),
                  pl.BlockSpec((tm, tn), lambda j, i: (i, j)),
                  pl.BlockSpec((2, tn), lambda j, i: (0, j))],
        out_specs=pl.BlockSpec((tm, tn), lambda j, i: (i, j)),
        scratch_shapes=[pltpu.VMEM((k, tn), BF16)],
        compiler_params=pltpu.CompilerParams(
            dimension_semantics=("arbitrary", "arbitrary"), vmem_limit_bytes=_vmem_limit(need)),
        name="out_proj_residual",
    )(a, w, res, gate2)


def _hgrn_consts(rev):
    c, sub = HG_CHUNK, HG_SUB
    t = np.arange(c)
    if not rev:
        tri = t[None, :] <= t[:, None]
        mid = (t // sub) * sub + sub // 2 - 1
        mmid = t[None, :] <= mid[:, None]
    else:
        tri = t[None, :] >= t[:, None]
        mid = (t // sub) * sub + sub // 2
        mmid = t[None, :] >= mid[:, None]
    mcat = np.concatenate([tri, mmid, np.ones((c, c), bool)], axis=0).astype(np.float32)
    return jnp.asarray(mcat, BF16), jnp.asarray(tri.astype(np.float32), F32)


def _hgrn_kernel(q_ref, f_ref, v_ref, loglb_ref, l1mlb_ref, omlb_ref, mcat_ref, tri_ref,
                 o_ref, st_ref, *, rev, heads):
    hg = pl.program_id(0)
    j = pl.program_id(1)
    c, sub, dk = HG_CHUNK, HG_SUB, HG_HEAD_DIM
    nsub = c // sub

    @pl.when(j == 0)
    def _():
        st_ref[...] = jnp.zeros_like(st_ref)

    s = f_ref[...]
    e = jnp.exp(-jnp.abs(s))
    lse = jnp.log1p(e)
    log_sig = jnp.minimum(s, 0.0) - lse
    a = loglb_ref[...]
    cc = l1mlb_ref[...] + log_sig
    logf = jnp.maximum(a, cc) + jnp.log1p(jnp.exp(-jnp.abs(a - cc)))
    sig_neg = jnp.where(s >= 0.0, e, 1.0) / (1.0 + e)
    kk = omlb_ref[...] * sig_neg
    q = q_ref[...]

    ball3 = _dot3(mcat_ref[...], logf)
    b, bmid, btot = ball3[0:c], ball3[c:2 * c], ball3[2 * c:3 * c]
    ka = (kk * jnp.exp(bmid - b)).astype(BF16)
    qs = (q * jnp.exp(b)).astype(BF16)
    ks = (kk * jnp.exp(btot - b)).astype(BF16)
    etot = jnp.exp(btot[0:1, :])
    vb = v_ref[...].astype(BF16)
    causal = tri_ref[...] > 0.5
    zero_tile = jnp.zeros((sub, dk), BF16)

    for h in range(heads):
        sl = slice(h * dk, (h + 1) * dk)
        bh, qh, bmh, kah = b[:, sl], q[:, sl], bmid[:, sl], ka[:, sl]
        lhs_parts, rhs_rows = [], []
        for jb in range(nsub):
            r0, r1 = (jb * sub, c) if not rev else (0, (jb + 1) * sub)
            ref_row = bmh[jb * sub:jb * sub + 1, :]
            part = (qh[r0:r1] * jnp.exp(bh[r0:r1] - ref_row)).astype(BF16)
            pieces = []
            if r0 > 0:
                pieces.append(jnp.zeros((r0, dk), BF16))
            pieces.append(part)
            if r1 < c:
                pieces.append(jnp.zeros((c - r1, dk), BF16))
            lhs_parts.append(jnp.concatenate(pieces, axis=0) if len(pieces) > 1 else part)
            row = [zero_tile] * nsub
            row[jb] = kah[jb * sub:(jb + 1) * sub]
            rhs_rows.append(jnp.concatenate(row, axis=1))
        lhs = jnp.concatenate(lhs_parts, axis=1)
        rhs = jnp.concatenate(rhs_rows, axis=0)
        scores = lax.dot_general(lhs, rhs, (((1,), (1,)), ((), ())), preferred_element_type=F32)
        amat = jnp.where(causal, scores, 0.0).astype(BF16)
        hidx = hg * heads + h
        st = st_ref[hidx]
        out = jnp.dot(amat, vb[:, sl], preferred_element_type=F32)
        out = out + lax.dot_general(qs[:, sl], st.astype(BF16), (((1,), (1,)), ((), ())),
                                    preferred_element_type=F32)
        o_ref[:, sl] = out
        upd = lax.dot_general(vb[:, sl], ks[:, sl], (((0,), (0,)), ((), ())), preferred_element_type=F32)
        st_ref[hidx] = st * etot[:, sl] + upd


def _hgrn_scan(p_hg, f_seg, loglb, l1mlb, omlb, n_x, rev):
    t_all, w = p_hg.shape[0], p_hg.shape[1] // 5
    c = HG_CHUNK
    heads = HG_HEADS_PER_STEP
    bw = heads * HG_HEAD_DIM
    nseg = w // bw
    n_heads = w // HG_HEAD_DIM
    ncx, nct = n_x // c, t_all // c
    ncc = nct - ncx
    mcat, tri = _hgrn_consts(rev)

    def chunk(j):
        if rev:
            return jnp.where(j < ncc, nct - 1 - j, ncx - 1 - (j - ncc))
        return jnp.where(j < ncc, ncx + j, j - ncc)

    def seg_spec(seg):
        return pl.BlockSpec((c, bw), lambda hg, j: (chunk(j), seg * nseg + hg))

    vec_spec = pl.BlockSpec((1, bw), lambda hg, j: (0, hg))
    return pl.pallas_call(
        functools.partial(_hgrn_kernel, rev=rev, heads=heads),
        out_shape=jax.ShapeDtypeStruct((t_all, w), F32),
        grid=(nseg, nct),
        in_specs=[seg_spec(0), seg_spec(f_seg), seg_spec(3), vec_spec, vec_spec, vec_spec,
                  pl.BlockSpec((3 * c, c), lambda hg, j: (0, 0)),
                  pl.BlockSpec((c, c), lambda hg, j: (0, 0))],
        out_specs=pl.BlockSpec((c, bw), lambda hg, j: (chunk(j), hg)),
        scratch_shapes=[pltpu.VMEM((n_heads, HG_HEAD_DIM, HG_HEAD_DIM), F32)],
        compiler_params=pltpu.CompilerParams(
            dimension_semantics=("arbitrary", "arbitrary"),
            vmem_limit_bytes=_vmem_limit(8 * c * bw * 4 + n_heads * HG_HEAD_DIM * HG_HEAD_DIM * 4)),
        name="hgrn2_scan_bw" if rev else "hgrn2_scan_fw",
    )(p_hg, p_hg, p_hg, loglb, l1mlb, omlb, mcat, tri)


def _shift_rows(u, down):
    n = u.shape[0]
    row = lax.broadcasted_iota(I32, u.shape, 0)
    if down:
        return jnp.where(row == 0, 0.0, pltpu.roll(u, 1, 0))
    return jnp.where(row == n - 1, 0.0, pltpu.roll(u, n - 1, 0))


def _to_scan_order(src_ref, dst_ref, rows):
    for col in range(GRID_W):
        dst_ref[col * rows:(col + 1) * rows, :] = src_ref[pl.ds(col, rows, stride=GRID_W), :]


def _conv_x_kernel(u_ref, w_ref, b_ref, o_ref, y_ref, *, rows):
    gw = GRID_W
    w0, w1, w2, bias = w_ref[0:1, :], w_ref[1:2, :], w_ref[2:3, :], b_ref[...]

    def slab(r):
        start = r * gw if isinstance(r, int) else pl.multiple_of(r * gw, gw)
        return pl.ds(start, gw)

    def emit(r, up, dn):
        y = w0 * up + w1 * u_ref[slab(r), :] + w2 * dn + bias
        y_ref[slab(r), :] = _silu(y)

    emit(0, _shift_rows(u_ref[slab(rows - 1), :], True), u_ref[slab(1), :])
    emit(rows - 1, u_ref[slab(rows - 2), :], _shift_rows(u_ref[slab(0), :], False))

    def body(r, carry):
        emit(r, u_ref[slab(r - 1), :], u_ref[slab(r + 1), :])
        return carry

    lax.fori_loop(1, rows - 1, body, 0)
    _to_scan_order(y_ref, o_ref, rows)


def _conv_c_kernel(u_ref, w_ref, b_ref, o_ref):
    u = u_ref[...]
    y = w_ref[0:1, :] * _shift_rows(u, True) + w_ref[1:2, :] * u + w_ref[2:3, :] * _shift_rows(u, False)
    o_ref[...] = _silu(y + b_ref[...])


def _permute_kernel(u_ref, o_ref, *, rows):
    _to_scan_order(u_ref, o_ref, rows)


def _unpermute_kernel(yf_ref, yb_ref, xs_ref, d_ref, o_ref, *, rows):
    d = d_ref[...]
    for col in range(GRID_W):
        sl = slice(col * rows, (col + 1) * rows)
        o_ref[pl.ds(col, rows, stride=GRID_W), :] = yf_ref[sl, :] + yb_ref[sl, :] + d * xs_ref[sl, :]


def _ssd_conv(p_xbc, conv_w, conv_b, n_x):
    t_all, f = p_xbc.shape
    rows = n_x // GRID_W
    fb = LANES
    blk = n_x * fb * 4
    xbc_x = pl.pallas_call(
        functools.partial(_conv_x_kernel, rows=rows),
        out_shape=jax.ShapeDtypeStruct((n_x, f), F32),
        grid=(f // fb,),
        in_specs=[pl.BlockSpec((n_x, fb), lambda k: (0, k)),
                  pl.BlockSpec((3, fb), lambda k: (0, k)),
                  pl.BlockSpec((1, fb), lambda k: (0, k))],
        out_specs=pl.BlockSpec((n_x, fb), lambda k: (0, k)),
        scratch_shapes=[pltpu.VMEM((n_x, fb), F32)],
        compiler_params=pltpu.CompilerParams(
            dimension_semantics=("arbitrary",), vmem_limit_bytes=_vmem_limit(5 * blk)),
        name="ssd_conv_latent",
    )(p_xbc, conv_w, conv_b)
    n_c = t_all - n_x
    fc = 1024
    xbc_c = pl.pallas_call(
        _conv_c_kernel,
        out_shape=jax.ShapeDtypeStruct((n_c, f), F32),
        grid=(f // fc,),
        in_specs=[pl.BlockSpec((n_c, fc), lambda k: (n_x // n_c, k)),
                  pl.BlockSpec((3, fc), lambda k: (0, k)),
                  pl.BlockSpec((1, fc), lambda k: (0, k))],
        out_specs=pl.BlockSpec((n_c, fc), lambda k: (0, k)),
        compiler_params=pltpu.CompilerParams(
            dimension_semantics=("arbitrary",), vmem_limit_bytes=_vmem_limit(4 * n_c * fc * 4)),
        name="ssd_conv_context",
    )(p_xbc, conv_w, conv_b)
    return xbc_x, xbc_c


def _permute_latent(u, n_x):
    rows = n_x // GRID_W
    return pl.pallas_call(
        functools.partial(_permute_kernel, rows=rows),
        out_shape=jax.ShapeDtypeStruct((n_x, u.shape[1]), F32),
        grid=(1,),
        in_specs=[pl.BlockSpec((n_x, u.shape[1]), lambda k: (0, 0))],
        out_specs=pl.BlockSpec((n_x, u.shape[1]), lambda k: (0, 0)),
        compiler_params=pltpu.CompilerParams(
            dimension_semantics=("arbitrary",), vmem_limit_bytes=_vmem_limit(4 * n_x * u.shape[1] * 4)),
        name="ssd_dt_to_scan_order",
    )(u)


def _unpermute_latent(y_fw, y_bw, xbc_x, d_row):
    n_x, w = y_fw.shape
    rows = n_x // GRID_W
    fb = LANES
    return pl.pallas_call(
        functools.partial(_unpermute_kernel, rows=rows),
        out_shape=jax.ShapeDtypeStruct((n_x, w), F32),
        grid=(w // fb,),
        in_specs=[pl.BlockSpec((n_x, fb), lambda k: (0, k)),
                  pl.BlockSpec((n_x, fb), lambda k: (0, k)),
                  pl.BlockSpec((n_x, fb), lambda k: (0, k)),
                  pl.BlockSpec((1, fb), lambda k: (0, k))],
        out_specs=pl.BlockSpec((n_x, fb), lambda k: (0, k)),
        compiler_params=pltpu.CompilerParams(
            dimension_semantics=("arbitrary",), vmem_limit_bytes=_vmem_limit(8 * n_x * fb * 4)),
        name="ssd_to_natural_order",
    )(y_fw, y_bw, xbc_x, d_row)


def _ssd_consts(rev):
    t = np.arange(SSD_CHUNK)
    tri = (t[None, :] >= t[:, None]) if rev else (t[None, :] <= t[:, None])
    return jnp.asarray(tri.astype(np.float32), BF16), jnp.asarray(tri.astype(np.float32), F32)


def _ssd_expanders(direction, n_heads):
    hpg = n_heads // SSD_GROUPS
    length = SSD_CHUNK
    e_lane = np.zeros((LANES, SSD_GROUPS * hpg * SSD_HEAD_DIM), np.float32)
    e_wide = np.zeros((LANES, SSD_GROUPS * hpg * length), np.float32)
    e_row = np.zeros((SSD_GROUPS * SUBLANES, LANES), np.float32)
    for g in range(SSD_GROUPS):
        for hh in range(hpg):
            col = direction * n_heads + g * hpg + hh
            e_lane[col, (g * hpg + hh) * SSD_HEAD_DIM:(g * hpg + hh + 1) * SSD_HEAD_DIM] = 1.0
            e_wide[col, (g * hpg + hh) * length:(g * hpg + hh + 1) * length] = 1.0
            e_row[g * SUBLANES + hh, col] = 1.0
    return jnp.asarray(e_lane, BF16), jnp.asarray(e_wide, BF16), jnp.asarray(e_row, BF16)


def _ssd_kernel(xs_ref, b_ref, c_ref, dt_ref, dtb_ref, a_ref, elane_ref, ewide_ref, erow_ref,
                tri01_ref, trif_ref, s0_ref, y_ref, sfin_ref, sgt_ref, *, rev, hpg):
    @pl.when(pl.program_id(1) == 0)
    def _():
        sgt_ref[...] = s0_ref[0]

    xs, bm, cm = xs_ref[...], b_ref[...], c_ref[...]
    length = xs.shape[0]
    p = SSD_HEAD_DIM
    dt_all = jax.nn.softplus(dt_ref[...] + dtb_ref[...])
    da_all = dt_all * a_ref[...]
    cum = _dot3(tri01_ref[...], da_all)
    tot = cum[0:1, :] if rev else cum[length - 1:length, :]
    dte = jnp.exp(tot - cum)
    ec = jnp.exp(cum)
    etot = jnp.broadcast_to(jnp.exp(tot), (SUBLANES, LANES))
    stacked = jnp.concatenate([dt_all, dte, ec, etot], axis=0)
    spread = _dot3_rhs01(stacked, elane_ref[...])
    dtb, dteb, ecb = spread[0:length], spread[length:2 * length], spread[2 * length:3 * length]
    cdb = spread[3 * length:3 * length + 1]
    cum_col = _dot3_rhs01(cum, ewide_ref[...])
    cum_row = _dot3(erow_ref[...], cum, (((1,), (1,)), ((), ())))

    xdt = xs * dtb
    cb = lax.dot_general(cm.astype(BF16), bm.astype(BF16), (((1,), (1,)), ((), ())),
                         preferred_element_type=F32)
    causal = trif_ref[...] > 0.5
    ys = []
    for hh in range(hpg):
        seg = cum_col[:, hh * length:(hh + 1) * length] - cum_row[hh:hh + 1, :]
        dec = jnp.exp(jnp.where(causal, seg, -jnp.inf))
        m = (cb * dec).astype(BF16)
        ys.append(jnp.dot(m, xdt[:, hh * p:(hh + 1) * p].astype(BF16), preferred_element_type=F32))
    sgt = sgt_ref[...]
    y_off = jnp.dot(cm.astype(BF16), sgt.astype(BF16), preferred_element_type=F32) * ecb
    y_ref[...] = jnp.concatenate(ys, axis=1) + y_off
    upd = lax.dot_general(bm.astype(BF16), (xdt * dteb).astype(BF16), (((0,), (0,)), ((), ())),
                          preferred_element_type=F32)
    sgt_new = sgt * cdb + upd
    sgt_ref[...] = sgt_new
    sfin_ref[0] = sgt_new


def _ssd_scan(xbc, dt, dt_bias, a_row, s0, n_heads, direction, stream_name):
    n = xbc.shape[0]
    w = n_heads * SSD_HEAD_DIM
    hpg = n_heads // SSD_GROUPS
    gw = hpg * SSD_HEAD_DIM
    length = SSD_CHUNK
    nch = n // length
    rev = direction == 1
    e_lane, e_wide, e_row = _ssd_expanders(direction, n_heads)
    tri01, trif = _ssd_consts(rev)
    nb0 = w // SSD_STATE

    def ch(j):
        return (nch - 1 - j) if rev else j

    const = lambda shape: pl.BlockSpec(shape, lambda g, j: (0,) * len(shape))
    return pl.pallas_call(
        functools.partial(_ssd_kernel, rev=rev, hpg=hpg),
        out_shape=(jax.ShapeDtypeStruct((n, w), F32),
                   jax.ShapeDtypeStruct((SSD_GROUPS, SSD_STATE, gw), F32)),
        grid=(SSD_GROUPS, nch),
        in_specs=[pl.BlockSpec((length, gw), lambda g, j: (ch(j), g)),
                  pl.BlockSpec((length, SSD_STATE), lambda g, j: (ch(j), nb0 + g)),
                  pl.BlockSpec((length, SSD_STATE), lambda g, j: (ch(j), nb0 + SSD_GROUPS + g)),
                  pl.BlockSpec((length, LANES), lambda g, j: (ch(j), 0)),
                  const((1, LANES)), const((1, LANES)),
                  pl.BlockSpec((LANES, gw), lambda g, j: (0, g)),
                  pl.BlockSpec((LANES, hpg * length), lambda g, j: (0, g)),
                  pl.BlockSpec((SUBLANES, LANES), lambda g, j: (g, 0)),
                  const((length, length)), const((length, length)),
                  pl.BlockSpec((1, SSD_STATE, gw), lambda g, j: (g, 0, 0))],
        out_specs=(pl.BlockSpec((length, gw), lambda g, j: (ch(j), g)),
                   pl.BlockSpec((1, SSD_STATE, gw), lambda g, j: (g, 0, 0))),
        scratch_shapes=[pltpu.VMEM((SSD_STATE, gw), F32)],
        compiler_params=pltpu.CompilerParams(dimension_semantics=("arbitrary", "arbitrary"),
                                             vmem_limit_bytes=_vmem_limit(16 * 1024 * 1024)),
        name="ssd_scan_%s_%s" % (stream_name, "bw" if rev else "fw"),
    )(xbc, xbc, xbc, dt, dt_bias, a_row, e_lane, e_wide, e_row, tri01, trif, s0)


def _readout_kernel(of_ref, ob_ref, g_ref, hw_ref, yx_ref, ycf_ref, ycb_ref, xsc_ref, z_ref, d_ref, sw_ref,
                    o_ref, *, n_xblk, hg_w, ssd_group):
    i = pl.program_id(0)
    o = of_ref[...] + ob_ref[...]
    gate = _silu(g_ref[...])
    hw = hw_ref[...]
    for h in range(hg_w // HG_HEAD_DIM):
        sl = slice(h * HG_HEAD_DIM, (h + 1) * HG_HEAD_DIM)
        oh = o[:, sl]
        ms = jnp.mean(oh * oh, axis=-1, keepdims=True)
        o_ref[:, sl] = (oh * lax.rsqrt(ms + EPS) * hw[:, sl] * gate[:, sl]).astype(o_ref.dtype)

    y_c = ycf_ref[...] + ycb_ref[...] + d_ref[...] * xsc_ref[...]
    y = jnp.where(i < n_xblk, yx_ref[...], y_c) * _silu(z_ref[...])
    sw = sw_ref[...]
    for g in range(y.shape[1] // ssd_group):
        sl = slice(g * ssd_group, (g + 1) * ssd_group)
        yg = y[:, sl]
        ms = jnp.mean(yg * yg, axis=-1, keepdims=True)
        o_ref[:, hg_w + g * ssd_group:hg_w + (g + 1) * ssd_group] = (
            yg * lax.rsqrt(ms + EPS) * sw[:, sl]).astype(o_ref.dtype)


def _readout(o_fw, o_bw, p_hg, hg_norm_w, y_x, yc_fw, yc_bw, xbc_c, p_z, d_row, ssd_norm_w, n_x):
    t_all, hg_w = o_fw.shape
    sw_w = p_z.shape[1]
    n_c = t_all - n_x
    tm = n_c
    n_xblk = n_x // tm
    xi = lambda i: (jnp.minimum(i, n_xblk - 1), 0)
    ci = lambda i: (0, 0)
    need = 2 * tm * 4 * (3 * hg_w + 5 * sw_w) + 2 * tm * (hg_w + sw_w) * 2 + 4 * tm * (hg_w + sw_w) * 4
    return pl.pallas_call(
        functools.partial(_readout_kernel, n_xblk=n_xblk, hg_w=hg_w, ssd_group=sw_w // SSD_GROUPS),
        out_shape=jax.ShapeDtypeStruct((t_all, hg_w + sw_w), BF16),
        grid=(t_all // tm,),
        in_specs=[pl.BlockSpec((tm, hg_w), lambda i: (i, 0)),
                  pl.BlockSpec((tm, hg_w), lambda i: (i, 0)),
                  pl.BlockSpec((tm, hg_w), lambda i: (i, 4)),
                  pl.BlockSpec((1, hg_w), lambda i: (0, 0)),
                  pl.BlockSpec((tm, sw_w), xi),
                  pl.BlockSpec((tm, sw_w), ci), pl.BlockSpec((tm, sw_w), ci), pl.BlockSpec((tm, sw_w), ci),
                  pl.BlockSpec((tm, sw_w), lambda i: (i, 0)),
                  pl.BlockSpec((1, sw_w), lambda i: (0, 0)),
                  pl.BlockSpec((1, sw_w), lambda i: (0, 0))],
        out_specs=pl.BlockSpec((tm, hg_w + sw_w), lambda i: (i, 0)),
        compiler_params=pltpu.CompilerParams(dimension_semantics=("arbitrary",),
                                             vmem_limit_bytes=_vmem_limit(need)),
        name="mixer_readout",
    )(o_fw, o_bw, p_hg, hg_norm_w, y_x, yc_fw, yc_bw, xbc_c, p_z, d_row, ssd_norm_w)


def _router_kernel(f_ref, w_ref, b_ref, id_ref, wt_ref):
    logits = jnp.dot(f_ref[...], w_ref[...], preferred_element_type=F32,
                     precision=lax.Precision.HIGHEST) + b_ref[...]
    tm = logits.shape[0]
    lane_i = lax.broadcasted_iota(I32, (tm, LANES), 1)
    lane = lane_i.astype(F32)
    neg = -jnp.inf
    big = float(LANES)
    is_g = lane_i < MOE_GROUPS
    gl = jnp.where(is_g, logits, neg)
    gmax = jnp.max(gl, axis=-1, keepdims=True)
    gsum = jnp.sum(jnp.where(is_g, jnp.exp(gl - gmax), 0.0), axis=-1, keepdims=True)
    p_top = 1.0 / gsum
    g_idx = jnp.min(jnp.where(gl == gmax, lane, big), axis=-1, keepdims=True)
    e_lo = MOE_GROUPS + g_idx * EXPERTS_PER_GROUP
    in_grp = (lane >= e_lo) & (lane < e_lo + EXPERTS_PER_GROUP)
    el = jnp.where(in_grp, logits, neg)
    m1 = jnp.max(el, axis=-1, keepdims=True)
    i1 = jnp.min(jnp.where(in_grp & (el == m1), lane, big), axis=-1, keepdims=True)
    el2 = jnp.where(lane == i1, neg, el)
    m2 = jnp.max(el2, axis=-1, keepdims=True)
    i2 = jnp.min(jnp.where(in_grp & (lane != i1) & (el2 == m2), lane, big), axis=-1, keepdims=True)
    e2 = jnp.exp(m2 - m1)
    w1 = p_top / (1.0 + e2)
    w2 = p_top * e2 / (1.0 + e2)
    ids = jnp.where(lane_i == 0, i1 - MOE_GROUPS, jnp.where(lane_i == 1, i2 - MOE_GROUPS, 0.0))
    id_ref[...] = ids.astype(I32)
    wt_ref[...] = jnp.where(lane_i == 0, w1, jnp.where(lane_i == 1, w2, 0.0))


def _router(f, w_r, b_r):
    n, d = f.shape
    tm = 256
    return pl.pallas_call(
        _router_kernel,
        out_shape=(jax.ShapeDtypeStruct((n, LANES), I32), jax.ShapeDtypeStruct((n, LANES), F32)),
        grid=(n // tm,),
        in_specs=[pl.BlockSpec((tm, d), lambda i: (i, 0)),
                  pl.BlockSpec((d, LANES), lambda i: (0, 0)),
                  pl.BlockSpec((1, LANES), lambda i: (0, 0))],
        out_specs=(pl.BlockSpec((tm, LANES), lambda i: (i, 0)), pl.BlockSpec((tm, LANES), lambda i: (i, 0))),
        compiler_params=pltpu.CompilerParams(dimension_semantics=("arbitrary",),
                                             vmem_limit_bytes=_vmem_limit(2 * tm * d * 4 + 4 * d * LANES * 4)),
        name="moe_router",
    )(f, w_r, b_r)


def _gather_kernel(idx_ref, src_ref, o_ref, buf_ref, sem_ref, *, rows):
    base = pl.program_id(0) * rows

    def copy(r):
        return pltpu.make_async_copy(src_ref.at[pl.ds(idx_ref[base + r], 1), :],
                                     buf_ref.at[pl.ds(r, 1), :], sem_ref.at[0])

    def start(r, carry):
        copy(r).start()
        return carry

    def wait(r, carry):
        copy(r).wait()
        return carry

    lax.fori_loop(0, rows, start, 0)
    lax.fori_loop(0, rows, wait, 0)
    o_ref[...] = buf_ref[...].astype(o_ref.dtype)


def _gather_rows(idx, src, n_out):
    d = src.shape[1]
    rows = MOE_ROWS
    return pl.pallas_call(
        functools.partial(_gather_kernel, rows=rows),
        out_shape=jax.ShapeDtypeStruct((n_out, d), BF16),
        grid_spec=pltpu.PrefetchScalarGridSpec(
            num_scalar_prefetch=1, grid=(n_out // rows,),
            in_specs=[pl.BlockSpec(memory_space=pl.ANY)],
            out_specs=pl.BlockSpec((rows, d), lambda i, idx: (i, 0)),
            scratch_shapes=[pltpu.VMEM((rows, d), F32), pltpu.SemaphoreType.DMA((1,))]),
        compiler_params=pltpu.CompilerParams(dimension_semantics=("arbitrary",),
                                             vmem_limit_bytes=_vmem_limit(rows * d * 8)),
        name="moe_gather",
    )(idx, src)


def _expert_kernel(blk_e_ref, n_used_ref, x_ref, wg_ref, wu_ref, wd_ref, o_ref):
    b = pl.program_id(0)

    @pl.when(b < n_used_ref[0])
    def _():
        x = x_ref[...]
        hg = jnp.dot(x, wg_ref[...], preferred_element_type=F32)
        hu = jnp.dot(x, wu_ref[...], preferred_element_type=F32)
        h = (_silu(hg) * hu).astype(BF16)
        o_ref[...] = jnp.dot(h, wd_ref[...], preferred_element_type=F32)

    @pl.when(b >= n_used_ref[0])
    def _():
        o_ref[...] = jnp.zeros_like(o_ref)


def _expert_mlp(blk_e, n_used, xb, wg, wu, wd):
    nr, d = xb.shape
    hid = wg.shape[2]
    rows = MOE_ROWS
    need = 2 * (rows * d * 2 + 3 * d * hid * 2 + rows * d * 4) + 4 * rows * hid * 4
    return pl.pallas_call(
        _expert_kernel,
        out_shape=jax.ShapeDtypeStruct((nr, d), F32),
        grid_spec=pltpu.PrefetchScalarGridSpec(
            num_scalar_prefetch=2, grid=(nr // rows,),
            in_specs=[pl.BlockSpec((rows, d), lambda b, be, nu: (b, 0)),
                      pl.BlockSpec((None, d, hid), lambda b, be, nu: (be[b], 0, 0)),
                      pl.BlockSpec((None, d, hid), lambda b, be, nu: (be[b], 0, 0)),
                      pl.BlockSpec((None, hid, d), lambda b, be, nu: (be[b], 0, 0))],
            out_specs=pl.BlockSpec((rows, d), lambda b, be, nu: (b, 0))),
        compiler_params=pltpu.CompilerParams(dimension_semantics=("arbitrary",),
                                             vmem_limit_bytes=_vmem_limit(need)),
        name="moe_expert_mlp",
    )(blk_e, n_used, xb, wg, wu, wd)


def _combine_kernel(d0_ref, d1_ref, yb_ref, x_ref, wt_ref, g_ref, nw_ref, o_ref, buf_ref, sem_ref,
                    *, rows, n_x, final_norm):
    i = pl.program_id(0)
    base = i * rows

    def copies(r):
        return (pltpu.make_async_copy(yb_ref.at[pl.ds(d0_ref[base + r], 1), :],
                                      buf_ref.at[0, pl.ds(r, 1), :], sem_ref.at[0]),
                pltpu.make_async_copy(yb_ref.at[pl.ds(d1_ref[base + r], 1), :],
                                      buf_ref.at[1, pl.ds(r, 1), :], sem_ref.at[1]))

    def start(r, carry):
        for cp in copies(r):
            cp.start()
        return carry

    def wait(r, carry):
        for cp in copies(r):
            cp.wait()
        return carry

    lax.fori_loop(0, rows, start, 0)
    lax.fori_loop(0, rows, wait, 0)
    wt = wt_ref[...]
    y = buf_ref[0] * wt[:, 0:1] + buf_ref[1] * wt[:, 1:2]
    out = x_ref[...] + _row_select(i, rows, n_x, g_ref) * y
    if final_norm:
        ms = jnp.mean(out * out, axis=-1, keepdims=True)
        out = out * lax.rsqrt(ms + EPS) * nw_ref[...]
    o_ref[...] = out


def _combine(dest0, dest1, yb, x_stream, wts, gate2, norm_w, n_rows, n_x, final_norm):
    d = yb.shape[1]
    rows = MOE_ROWS
    return pl.pallas_call(
        functools.partial(_combine_kernel, rows=rows, n_x=n_x, final_norm=final_norm),
        out_shape=jax.ShapeDtypeStruct((n_rows, d), F32),
        grid_spec=pltpu.PrefetchScalarGridSpec(
            num_scalar_prefetch=2, grid=(n_rows // rows,),
            in_specs=[pl.BlockSpec(memory_space=pl.ANY),
                      pl.BlockSpec((rows, d), lambda i, a, b: (i, 0)),
                      pl.BlockSpec((rows, LANES), lambda i, a, b: (i, 0)),
                      pl.BlockSpec((2, d), lambda i, a, b: (0, 0)),
                      pl.BlockSpec((1, d), lambda i, a, b: (0, 0))],
            out_specs=pl.BlockSpec((rows, d), lambda i, a, b: (i, 0)),
            scratch_shapes=[pltpu.VMEM((2, rows, d), F32), pltpu.SemaphoreType.DMA((2,))]),
        compiler_params=pltpu.CompilerParams(dimension_semantics=("arbitrary",),
                                             vmem_limit_bytes=_vmem_limit(6 * rows * d * 4 + 2 * rows * d * 4)),
        name="moe_combine",
    )(dest0, dest1, yb, x_stream, wts, gate2, norm_w)


def _dispatch_tables(eid, n_tok):
    n_assign = n_tok * TOP_K
    flat_e = eid.reshape(-1)
    order = jnp.argsort(flat_e, stable=True)
    se = flat_e[order]
    counts = jnp.bincount(flat_e, length=N_EXPERTS)
    padded = (counts + MOE_ROWS - 1) // MOE_ROWS * MOE_ROWS
    pad_end = jnp.cumsum(padded)
    pad_start = pad_end - padded
    start = jnp.cumsum(counts) - counts
    dest_sorted = (pad_start[se] + jnp.arange(n_assign, dtype=I32) - start[se]).astype(I32)
    n_blocks = -(-n_assign // MOE_ROWS) + N_EXPERTS
    buf_tok = jnp.zeros((n_blocks * MOE_ROWS,), I32).at[dest_sorted].set((order // TOP_K).astype(I32))
    dest = jnp.zeros((n_assign,), I32).at[order].set(dest_sorted).reshape(n_tok, TOP_K)
    blk_e = jnp.minimum(jnp.searchsorted(pad_end, jnp.arange(n_blocks, dtype=I32) * MOE_ROWS, side='right'),
                        N_EXPERTS - 1).astype(I32)
    n_used = (pad_end[-1] // MOE_ROWS).astype(I32).reshape(1)
    return buf_tok, dest, blk_e, n_used, n_blocks


def _hier_moe(stream, n_rows, n_x, norm_w, shift2, scale2, gate2, w_r, b_r, wg, wu, wd, final_w):
    f = _norm_mod(stream, norm_w, shift2, scale2, n_rows, n_x, F32)
    ids, wts = _router(f, w_r, b_r)
    buf_tok, dest, blk_e, n_used, n_blocks = _dispatch_tables(ids[:, :TOP_K], n_rows)
    xb = _gather_rows(buf_tok, f, n_blocks * MOE_ROWS)
    yb = _expert_mlp(blk_e, n_used, xb, wg, wu, wd)
    final_norm = final_w is not None
    nw = final_w if final_norm else norm_w
    return _combine(dest[:, 0], dest[:, 1], yb, stream, wts, gate2, nw, n_rows, n_x, final_norm)


def kernel(x, c, ctx, c_ctx, w_mod, b_mod, norm1_w, w_in, hgrn_lower_bounds, hgrn_norm_w, ssd_conv_w, ssd_conv_b,
           ssd_dt_bias, ssd_a_log, ssd_d, ssd_norm_w, w_out, norm2_w, w_group_router, b_group_router,
           w_expert_router, b_expert_router, w_gate, w_up, w_down, final_norm_w):
    bsz, seq, d = x.shape
    assert bsz == 1
    n_ctx = ctx.shape[1]
    depth = w_mod.shape[0]
    t_all = seq + n_ctx
    hg_w = d // 2
    ssd_w = d - hg_w
    n_ssd_heads = ssd_w // SSD_HEAD_DIM
    hg_cols = 5 * hg_w
    conv_dim = ssd_w + 2 * SSD_GROUPS * SSD_STATE
    assert seq // GRID_W == SSD_CHUNK and n_ctx % SSD_CHUNK == 0 and seq % n_ctx == 0
    gw = n_ssd_heads // SSD_GROUPS * SSD_HEAD_DIM
    dt_pad = LANES - 2 * n_ssd_heads

    stream = jnp.concatenate([x[0], ctx[0]], axis=0)
    c2t = jnp.stack([c[0], c_ctx], axis=1)
    lb_soft = jax.nn.softmax(hgrn_lower_bounds.astype(F32), axis=0)
    lower_bounds = jnp.cumsum(lb_soft, axis=0) - lb_soft[0]
    s_zero = jnp.zeros((SSD_GROUPS, SSD_STATE, gw), F32)

    for l in range(depth):
        last = l == depth - 1
        mod = _mod_vectors(c2t, w_mod[l], b_mod[l][None, :]).reshape(2, MOD_CHUNKS, d)
        chunk = lambda k: mod[:, k, :]

        h = _norm_mod(stream, norm1_w[l][None, :], chunk(0), chunk(1), t_all, seq, BF16)
        w_in_l = w_in[l]
        p_hg = _matmul(h, w_in_l, 0, hg_cols)
        p_z = _matmul(h, w_in_l, hg_cols, ssd_w)
        p_xbc = _matmul(h, w_in_l, hg_cols + ssd_w, conv_dim)
        w_dt = jnp.pad(w_in_l[:, hg_cols + ssd_w + conv_dim:], ((0, 0), (0, dt_pad)))
        p_dt = _matmul(h, w_dt, 0, LANES)

        lb = lower_bounds[l]
        outs = []
        for direction in range(2):
            lbd = lb[direction][None, :]
            outs.append(_hgrn_scan(p_hg, 1 + direction, jnp.log(lbd), jnp.log1p(-lbd), 1.0 - lbd, seq,
                                   rev=direction == 1))
        o_fw, o_bw = outs

        xbc_x, xbc_c = _ssd_conv(p_xbc, ssd_conv_w[l], ssd_conv_b[l][None, :], seq)
        dt_bias = jnp.pad(ssd_dt_bias[l].reshape(1, -1), ((0, 0), (0, dt_pad)))
        a_row = jnp.pad(-jnp.exp(ssd_a_log[l].astype(F32)).reshape(1, -1), ((0, 0), (0, dt_pad)))
        dt_x = _permute_latent(p_dt, seq)
        dt_c = p_dt[seq:]
        ys_x, ys_c = [], []
        for direction in range(2):
            y_c, s_c = _ssd_scan(xbc_c, dt_c, dt_bias, a_row, s_zero, n_ssd_heads, direction, "context")
            y_x, _ = _ssd_scan(xbc_x, dt_x, dt_bias, a_row, s_c, n_ssd_heads, direction, "latent")
            ys_c.append(y_c)
            ys_x.append(y_x)
        d_row = jnp.repeat(ssd_d[l].astype(F32), SSD_HEAD_DIM)[None, :]
        y_x = _unpermute_latent(ys_x[0], ys_x[1], xbc_x, d_row)
        ab = _readout(o_fw, o_bw, p_hg, jnp.tile(hgrn_norm_w[l], hg_w // HG_HEAD_DIM)[None, :],
                      y_x, ys_c[0], ys_c[1], xbc_c, p_z, d_row, ssd_norm_w[l][None, :], seq)
        stream = _matmul_residual(ab, w_out[l], stream, chunk(2), seq)

        w_r = jnp.pad(jnp.concatenate([w_group_router[l], w_expert_router[l]], axis=1),
                      ((0, 0), (0, LANES - MOE_GROUPS - N_EXPERTS)))
        b_r = jnp.pad(jnp.concatenate([b_group_router[l], b_expert_router[l]]),
                      (0, LANES - MOE_GROUPS - N_EXPERTS))[None, :]
        n_rows = seq if last else t_all
        stream = _hier_moe(stream, n_rows, seq, norm2_w[l][None, :], chunk(3), chunk(4), chunk(5), w_r, b_r,
                           w_gate[l].astype(BF16), w_up[l].astype(BF16), w_down[l].astype(BF16),
                           final_norm_w[None, :] if last else None)
    return stream[:seq].reshape(bsz, seq, d)
```

```python
import functools

import numpy as np
import jax
import jax.numpy as jnp
from jax import lax
from jax.experimental import pallas as pl
from jax.experimental.pallas import tpu as pltpu

F32 = jnp.float32
BF16 = jnp.bfloat16
I32 = jnp.int32

EPS = 1e-6
GRID_W = 64
MOD_CHUNKS = 6

HG_HEAD_DIM = 128
SSD_HEAD_DIM = 64
SSD_GROUPS = 8
SSD_STATE = 128
MOE_GROUPS = 4
EXPERTS_PER_GROUP = 8
N_EXPERTS = MOE_GROUPS * EXPERTS_PER_GROUP
TOP_K = 2

LANES = 128
SUBLANES = 8
VMEM_BYTES_V7X = 64 * 1024 * 1024
HG_CHUNK = 64
HG_SUB = 16
HG_HEADS_PER_STEP = 8
SSD_CHUNK = 128
SSD_GROUPS_PER_STEP = 4
MOE_ROWS = 256


def _vmem_limit(nbytes):
    return int(min(max(nbytes * 3 // 2, 16 * 1024 * 1024), VMEM_BYTES_V7X - 8 * 1024 * 1024))


def _split3(x):
    h = x.astype(BF16)
    r = x - h.astype(F32)
    m = r.astype(BF16)
    lo = (r - m.astype(F32)).astype(BF16)
    return h, m, lo


def _dot3(a01, x, dims=None):
    out = None
    for part in _split3(x):
        if dims is None:
            t = jnp.dot(a01, part, preferred_element_type=F32)
        else:
            t = lax.dot_general(a01, part, dims, preferred_element_type=F32)
        out = t if out is None else out + t
    return out


def _dot3_rhs01(x, b01):
    out = None
    for part in _split3(x):
        t = jnp.dot(part, b01, preferred_element_type=F32)
        out = t if out is None else out + t
    return out


def _silu(x):
    return x * jax.nn.sigmoid(x)


def _pack_bf16_pairs(v):
    half = v.shape[1] // 2
    rb = v.astype(BF16).astype(F32)
    hi = lax.bitcast_convert_type(rb[:, :half], jnp.uint32)
    lo = lax.shift_right_logical(lax.bitcast_convert_type(rb[:, half:], jnp.uint32), jnp.uint32(16))
    return hi | lo


def _unpack_bf16_pairs(pk):
    first = lax.bitcast_convert_type(pk & jnp.uint32(0xFFFF0000), F32)
    second = lax.bitcast_convert_type(lax.shift_left(pk, jnp.uint32(16)), F32)
    return first, second


def _mod_kernel(c_ref, w_ref, b_ref, o_ref, acc_ref):
    k = pl.program_id(1)

    @pl.when(k == 0)
    def _():
        acc_ref[...] = jnp.zeros_like(acc_ref)

    c = c_ref[...]
    s = _silu(c)
    w = w_ref[...]
    tk, tn = w.shape
    for r in range(2):
        prod = w * s[:, r:r + 1]
        acc_ref[r] += jnp.sum(prod.reshape(tk // SUBLANES, SUBLANES, tn), axis=0)

    @pl.when(k == pl.num_programs(1) - 1)
    def _():
        o_ref[...] = jnp.sum(acc_ref[...], axis=1) + b_ref[...]


def _mod_vectors(c2t, w_all, layer, b):
    _, d, n = w_all.shape
    tk, tn = min(1024, d), min(2048, n)
    return pl.pallas_call(
        _mod_kernel,
        out_shape=jax.ShapeDtypeStruct((2, n), F32),
        grid=(n // tn, d // tk),
        in_specs=[pl.BlockSpec((tk, 2), lambda j, k: (k, 0)),
                  pl.BlockSpec((None, tk, tn), lambda j, k: (layer, k, j)),
                  pl.BlockSpec((1, tn), lambda j, k: (0, j))],
        out_specs=pl.BlockSpec((2, tn), lambda j, k: (0, j)),
        scratch_shapes=[pltpu.VMEM((2, SUBLANES, tn), F32)],
        compiler_params=pltpu.CompilerParams(
            dimension_semantics=("arbitrary", "arbitrary"),
            vmem_limit_bytes=_vmem_limit(2 * tk * tn * 4 + 4 * tk * tn)),
        name="adaln_matvec",
    )(c2t, w_all, b)


def _row_select(i, tm, n_x, ref):
    row = i * tm + lax.broadcasted_iota(I32, (tm, 1), 0)
    return jnp.where(row < n_x, ref[0:1, :], ref[1:2, :])


def _norm_mod_kernel(x_ref, w_ref, sh_ref, sc_ref, o_ref, *, n_x, tm):
    i = pl.program_id(0)
    x = x_ref[...]
    ms = jnp.mean(x * x, axis=-1, keepdims=True)
    y = x * lax.rsqrt(ms + EPS) * w_ref[...]
    sc = _row_select(i, tm, n_x, sc_ref)
    sh = _row_select(i, tm, n_x, sh_ref)
    o_ref[...] = (y * (1.0 + sc) + sh).astype(o_ref.dtype)


def _norm_mod(x, w, shift2, scale2, n_rows, n_x, out_dtype):
    d = x.shape[1]
    tm = 256
    return pl.pallas_call(
        functools.partial(_norm_mod_kernel, n_x=n_x, tm=tm),
        out_shape=jax.ShapeDtypeStruct((n_rows, d), out_dtype),
        grid=(n_rows // tm,),
        in_specs=[pl.BlockSpec((tm, d), lambda i: (i, 0)),
                  pl.BlockSpec((1, d), lambda i: (0, 0)),
                  pl.BlockSpec((2, d), lambda i: (0, 0)),
                  pl.BlockSpec((2, d), lambda i: (0, 0))],
        out_specs=pl.BlockSpec((tm, d), lambda i: (i, 0)),
        compiler_params=pltpu.CompilerParams(
            dimension_semantics=("arbitrary",),
            vmem_limit_bytes=_vmem_limit(4 * tm * d * 4 + 4 * tm * d * 4)),
        name="norm_modulate",
    )(x, w, shift2, scale2)


def _mm_kernel(a_ref, w_ref, o_ref, wb_ref):
    @pl.when(pl.program_id(1) == 0)
    def _():
        wb_ref[...] = w_ref[...].astype(BF16)

    o_ref[...] = jnp.dot(a_ref[...], wb_ref[...], preferred_element_type=F32)


def _mm_res_kernel(a_ref, w_ref, r_ref, g_ref, o_ref, wb_ref, *, n_x, tm):
    i = pl.program_id(1)

    @pl.when(i == 0)
    def _():
        wb_ref[...] = w_ref[...].astype(BF16)

    acc = jnp.dot(a_ref[...], wb_ref[...], preferred_element_type=F32)
    o_ref[...] = r_ref[...] + _row_select(i, tm, n_x, g_ref) * acc


def _mm_tiles(m, k, n):
    tm = 1056 if m % 1056 == 0 else (1024 if m % 1024 == 0 else m)
    tn = 512 if n % 512 == 0 else n
    need = 2 * k * tn * 4 + k * tn * 2 + 2 * tm * k * 2 + 4 * tm * tn * 4
    return tm, tn, need


def _matmul(a, w_all, layer, col_off, n):
    m, k = a.shape
    tm, tn, need = _mm_tiles(m, k, n)
    off = col_off // tn
    assert off * tn == col_off
    return pl.pallas_call(
        _mm_kernel,
        out_shape=jax.ShapeDtypeStruct((m, n), F32),
        grid=(n // tn, m // tm),
        in_specs=[pl.BlockSpec((tm, k), lambda j, i: (i, 0)),
                  pl.BlockSpec((None, k, tn), lambda j, i: (layer, 0, j + off))],
        out_specs=pl.BlockSpec((tm, tn), lambda j, i: (i, j)),
        scratch_shapes=[pltpu.VMEM((k, tn), BF16)],
        compiler_params=pltpu.CompilerParams(
            dimension_semantics=("arbitrary", "arbitrary"), vmem_limit_bytes=_vmem_limit(need)),
        name="proj_matmul",
    )(a, w_all)


def _matmul_residual(a, w_all, layer, res, gate2, n_x):
    m, k = a.shape
    n = w_all.shape[2]
    tm, tn, need = _mm_tiles(m, k, n)
    return pl.pallas_call(
        functools.partial(_mm_res_kernel, n_x=n_x, tm=tm),
        out_shape=jax.ShapeDtypeStruct((m, n), F32),
        grid=(n // tn, m // tm),
        in_specs=[pl.BlockSpec((tm, k), lambda j, i: (i, 0)),
                  pl.BlockSpec((None, k, tn), lambda j, i: (layer, 0, j)),
                  pl.BlockSpec((tm, tn), lambda j, i: (i, j)),
                  pl.BlockSpec((2, tn), lambda j, i: (0, j))],
        out_specs=pl.BlockSpec((tm, tn), lambda j, i: (i, j)),
        scratch_shapes=[pltpu.VMEM((k, tn), BF16)],
        compiler_params=pltpu.CompilerParams(
            dimension_semantics=("arbitrary", "arbitrary"), vmem_limit_bytes=_vmem_limit(need)),
        name="out_proj_residual",
    )(a, w_all, res, gate2)


def _hgrn_consts(rev):
    c, sub = HG_CHUNK, HG_SUB
    t = np.arange(c)
    if not rev:
        tri = t[None, :] <= t[:, None]
        mid = (t // sub) * sub + sub // 2 - 1
        mmid = t[None, :] <= mid[:, None]
    else:
        tri = t[None, :] >= t[:, None]
        mid = (t // sub) * sub + sub // 2
        mmid = t[None, :] >= mid[:, None]
    mcat = np.concatenate([tri, mmid, np.ones((c, c), bool)], axis=0).astype(np.float32)
    return jnp.asarray(mcat, BF16), jnp.asarray(tri.astype(np.float32), F32)


def _hgrn_kernel(q_ref, f_ref, v_ref, loglb_ref, l1mlb_ref, omlb_ref, mcat_ref, tri_ref,
                 o_ref, st_ref, *, rev, heads):
    hg = pl.program_id(0)
    j = pl.program_id(1)
    c, sub, dk = HG_CHUNK, HG_SUB, HG_HEAD_DIM
    nsub = c // sub

    @pl.when(j == 0)
    def _():
        st_ref[...] = jnp.zeros_like(st_ref)

    s = f_ref[...]
    e = jnp.exp(-jnp.abs(s))
    lse = jnp.log1p(e)
    log_sig = jnp.minimum(s, 0.0) - lse
    a = loglb_ref[...]
    cc = l1mlb_ref[...] + log_sig
    logf = jnp.maximum(a, cc) + jnp.log1p(jnp.exp(-jnp.abs(a - cc)))
    sig_neg = jnp.where(s >= 0.0, e, 1.0) / (1.0 + e)
    kk = omlb_ref[...] * sig_neg
    q = q_ref[...]

    ball3 = _dot3(mcat_ref[...], logf)
    b, bmid, btot = ball3[0:c], ball3[c:2 * c], ball3[2 * c:3 * c]
    ka = (kk * jnp.exp(bmid - b)).astype(BF16)
    qs = (q * jnp.exp(b)).astype(BF16)
    ks = (kk * jnp.exp(btot - b)).astype(BF16)
    etot = jnp.exp(btot[0:1, :])
    vb = v_ref[...].astype(BF16)
    causal = tri_ref[...] > 0.5
    zero_tile = jnp.zeros((sub, dk), BF16)

    for h in range(heads):
        sl = slice(h * dk, (h + 1) * dk)
        bh, qh, bmh, kah = b[:, sl], q[:, sl], bmid[:, sl], ka[:, sl]
        lhs_parts, rhs_rows = [], []
        for jb in range(nsub):
            r0, r1 = (jb * sub, c) if not rev else (0, (jb + 1) * sub)
            ref_row = bmh[jb * sub:jb * sub + 1, :]
            part = (qh[r0:r1] * jnp.exp(bh[r0:r1] - ref_row)).astype(BF16)
            pieces = []
            if r0 > 0:
                pieces.append(jnp.zeros((r0, dk), BF16))
            pieces.append(part)
            if r1 < c:
                pieces.append(jnp.zeros((c - r1, dk), BF16))
            lhs_parts.append(jnp.concatenate(pieces, axis=0) if len(pieces) > 1 else part)
            row = [zero_tile] * nsub
            row[jb] = kah[jb * sub:(jb + 1) * sub]
            rhs_rows.append(jnp.concatenate(row, axis=1))
        lhs = jnp.concatenate(lhs_parts, axis=1)
        rhs = jnp.concatenate(rhs_rows, axis=0)
        scores = lax.dot_general(lhs, rhs, (((1,), (1,)), ((), ())), preferred_element_type=F32)
        amat = jnp.where(causal, scores, 0.0).astype(BF16)
        hidx = hg * heads + h
        st = st_ref[hidx]
        out = jnp.dot(amat, vb[:, sl], preferred_element_type=F32)
        out = out + lax.dot_general(qs[:, sl], st.astype(BF16), (((1,), (1,)), ((), ())),
                                    preferred_element_type=F32)
        o_ref[:, sl] = out
        upd = lax.dot_general(vb[:, sl], ks[:, sl], (((0,), (0,)), ((), ())), preferred_element_type=F32)
        st_ref[hidx] = st * etot[:, sl] + upd


def _hgrn_scan(p_hg, f_seg, loglb, l1mlb, omlb, n_x, rev):
    t_all, w = p_hg.shape[0], p_hg.shape[1] // 5
    c = HG_CHUNK
    heads = HG_HEADS_PER_STEP
    bw = heads * HG_HEAD_DIM
    nseg = w // bw
    n_heads = w // HG_HEAD_DIM
    ncx, nct = n_x // c, t_all // c
    ncc = nct - ncx
    mcat, tri = _hgrn_consts(rev)

    def chunk(j):
        if rev:
            return jnp.where(j < ncc, nct - 1 - j, ncx - 1 - (j - ncc))
        return jnp.where(j < ncc, ncx + j, j - ncc)

    def seg_spec(seg):
        return pl.BlockSpec((c, bw), lambda hg, j: (chunk(j), seg * nseg + hg))

    vec_spec = pl.BlockSpec((1, bw), lambda hg, j: (0, hg))
    return pl.pallas_call(
        functools.partial(_hgrn_kernel, rev=rev, heads=heads),
        out_shape=jax.ShapeDtypeStruct((t_all, w), F32),
        grid=(nseg, nct),
        in_specs=[seg_spec(0), seg_spec(f_seg), seg_spec(3), vec_spec, vec_spec, vec_spec,
                  pl.BlockSpec((3 * c, c), lambda hg, j: (0, 0)),
                  pl.BlockSpec((c, c), lambda hg, j: (0, 0))],
        out_specs=pl.BlockSpec((c, bw), lambda hg, j: (chunk(j), hg)),
        scratch_shapes=[pltpu.VMEM((n_heads, HG_HEAD_DIM, HG_HEAD_DIM), F32)],
        compiler_params=pltpu.CompilerParams(
            dimension_semantics=("arbitrary", "arbitrary"),
            vmem_limit_bytes=_vmem_limit(8 * c * bw * 4 + n_heads * HG_HEAD_DIM * HG_HEAD_DIM * 4)),
        name="hgrn2_scan_bw" if rev else "hgrn2_scan_fw",
    )(p_hg, p_hg, p_hg, loglb, l1mlb, omlb, mcat, tri)


def _shift_rows(u, down):
    n = u.shape[0]
    row = lax.broadcasted_iota(I32, u.shape, 0)
    if down:
        return jnp.where(row == 0, 0.0, pltpu.roll(u, 1, 0))
    return jnp.where(row == n - 1, 0.0, pltpu.roll(u, n - 1, 0))


def _to_scan_order(src_ref, dst_ref, rows):
    for col in range(GRID_W):
        dst_ref[col * rows:(col + 1) * rows, :] = src_ref[pl.ds(col, rows, stride=GRID_W), :]


def _conv_x_kernel(u_ref, w_ref, b_ref, o_ref, y_ref, *, rows):
    gw = GRID_W
    w0, w1, w2, bias = w_ref[0:1, :], w_ref[1:2, :], w_ref[2:3, :], b_ref[...]

    def slab(r):
        start = r * gw if isinstance(r, int) else pl.multiple_of(r * gw, gw)
        return pl.ds(start, gw)

    def emit(r, up, dn):
        y = w0 * up + w1 * u_ref[slab(r), :] + w2 * dn + bias
        y_ref[slab(r), :] = _silu(y)

    emit(0, _shift_rows(u_ref[slab(rows - 1), :], True), u_ref[slab(1), :])
    emit(rows - 1, u_ref[slab(rows - 2), :], _shift_rows(u_ref[slab(0), :], False))

    def body(r, carry):
        emit(r, u_ref[slab(r - 1), :], u_ref[slab(r + 1), :])
        return carry

    lax.fori_loop(1, rows - 1, body, 0)
    _to_scan_order(y_ref, o_ref, rows)


def _conv_c_kernel(u_ref, w_ref, b_ref, o_ref):
    u = u_ref[...]
    y = w_ref[0:1, :] * _shift_rows(u, True) + w_ref[1:2, :] * u + w_ref[2:3, :] * _shift_rows(u, False)
    o_ref[...] = _silu(y + b_ref[...])


def _permute_kernel(u_ref, o_ref, *, rows):
    _to_scan_order(u_ref, o_ref, rows)


def _unpermute_kernel(yf_ref, yb_ref, xs_ref, d_ref, o_ref, *, rows):
    d = d_ref[...]
    for col in range(GRID_W):
        sl = slice(col * rows, (col + 1) * rows)
        o_ref[pl.ds(col, rows, stride=GRID_W), :] = yf_ref[sl, :] + yb_ref[sl, :] + d * xs_ref[sl, :]


def _ssd_conv(p_xbc, conv_w, conv_b, n_x):
    t_all, f = p_xbc.shape
    rows = n_x // GRID_W
    fb = LANES
    blk = n_x * fb * 4
    xbc_x = pl.pallas_call(
        functools.partial(_conv_x_kernel, rows=rows),
        out_shape=jax.ShapeDtypeStruct((n_x, f), F32),
        grid=(f // fb,),
        in_specs=[pl.BlockSpec((n_x, fb), lambda k: (0, k)),
                  pl.BlockSpec((3, fb), lambda k: (0, k)),
                  pl.BlockSpec((1, fb), lambda k: (0, k))],
        out_specs=pl.BlockSpec((n_x, fb), lambda k: (0, k)),
        scratch_shapes=[pltpu.VMEM((n_x, fb), F32)],
        compiler_params=pltpu.CompilerParams(
            dimension_semantics=("arbitrary",), vmem_limit_bytes=_vmem_limit(5 * blk)),
        name="ssd_conv_latent",
    )(p_xbc, conv_w, conv_b)
    n_c = t_all - n_x
    fc = 1024
    xbc_c = pl.pallas_call(
        _conv_c_kernel,
        out_shape=jax.ShapeDtypeStruct((n_c, f), F32),
        grid=(f // fc,),
        in_specs=[pl.BlockSpec((n_c, fc), lambda k: (n_x // n_c, k)),
                  pl.BlockSpec((3, fc), lambda k: (0, k)),
                  pl.BlockSpec((1, fc), lambda k: (0, k))],
        out_specs=pl.BlockSpec((n_c, fc), lambda k: (0, k)),
        compiler_params=pltpu.CompilerParams(
            dimension_semantics=("arbitrary",), vmem_limit_bytes=_vmem_limit(4 * n_c * fc * 4)),
        name="ssd_conv_context",
    )(p_xbc, conv_w, conv_b)
    return xbc_x, xbc_c


def _permute_latent(u, n_x):
    rows = n_x // GRID_W
    return pl.pallas_call(
        functools.partial(_permute_kernel, rows=rows),
        out_shape=jax.ShapeDtypeStruct((n_x, u.shape[1]), F32),
        grid=(1,),
        in_specs=[pl.BlockSpec((n_x, u.shape[1]), lambda k: (0, 0))],
        out_specs=pl.BlockSpec((n_x, u.shape[1]), lambda k: (0, 0)),
        compiler_params=pltpu.CompilerParams(
            dimension_semantics=("arbitrary",), vmem_limit_bytes=_vmem_limit(4 * n_x * u.shape[1] * 4)),
        name="ssd_dt_to_scan_order",
    )(u)


def _unpermute_latent(y_fw, y_bw, xbc_x, d_row):
    n_x, w = y_fw.shape
    rows = n_x // GRID_W
    fb = LANES
    return pl.pallas_call(
        functools.partial(_unpermute_kernel, rows=rows),
        out_shape=jax.ShapeDtypeStruct((n_x, w), F32),
        grid=(w // fb,),
        in_specs=[pl.BlockSpec((n_x, fb), lambda k: (0, k)),
                  pl.BlockSpec((n_x, fb), lambda k: (0, k)),
                  pl.BlockSpec((n_x, fb), lambda k: (0, k)),
                  pl.BlockSpec((1, fb), lambda k: (0, k))],
        out_specs=pl.BlockSpec((n_x, fb), lambda k: (0, k)),
        compiler_params=pltpu.CompilerParams(
            dimension_semantics=("arbitrary",), vmem_limit_bytes=_vmem_limit(8 * n_x * fb * 4)),
        name="ssd_to_natural_order",
    )(y_fw, y_bw, xbc_x, d_row)


def _ssd_consts(rev):
    t = np.arange(SSD_CHUNK)
    tri = (t[None, :] >= t[:, None]) if rev else (t[None, :] <= t[:, None])
    return jnp.asarray(tri.astype(np.float32), BF16), jnp.asarray(tri.astype(np.float32), F32)


def _ssd_expanders(direction, n_heads):
    hpg = n_heads // SSD_GROUPS
    length = SSD_CHUNK
    e_lane = np.zeros((LANES, SSD_GROUPS * hpg * SSD_HEAD_DIM), np.float32)
    e_wide = np.zeros((LANES, SSD_GROUPS * hpg * length), np.float32)
    e_row = np.zeros((SSD_GROUPS * SUBLANES, LANES), np.float32)
    for g in range(SSD_GROUPS):
        for hh in range(hpg):
            col = direction * n_heads + g * hpg + hh
            e_lane[col, (g * hpg + hh) * SSD_HEAD_DIM:(g * hpg + hh + 1) * SSD_HEAD_DIM] = 1.0
            e_wide[col, (g * hpg + hh) * length:(g * hpg + hh + 1) * length] = 1.0
            e_row[g * SUBLANES + hh, col] = 1.0
    return jnp.asarray(e_lane, BF16), jnp.asarray(e_wide, BF16), jnp.asarray(e_row, BF16)


def _ssd_kernel(xs_ref, b_ref, c_ref, dt_ref, dtb_ref, a_ref, elane_ref, ewide_ref, erow_ref,
                tri01_ref, trif_ref, s0_ref, y_ref, sfin_ref, sgt_ref, *, rev, hpg, ngs):
    @pl.when(pl.program_id(1) == 0)
    def _():
        sgt_ref[...] = s0_ref[...]

    length = xs_ref.shape[0]
    p = SSD_HEAD_DIM
    gw = hpg * p
    nst = SSD_STATE
    dt_all = jax.nn.softplus(dt_ref[...] + dtb_ref[...])
    da_all = dt_all * a_ref[...]
    cum = _dot3(tri01_ref[...], da_all)
    tot = cum[0:1, :] if rev else cum[length - 1:length, :]
    dte = jnp.exp(tot - cum)
    ec = jnp.exp(cum)
    etot = jnp.broadcast_to(jnp.exp(tot), (SUBLANES, LANES))
    stacked = jnp.concatenate([dt_all, dte, ec, etot], axis=0)
    spread = _dot3_rhs01(stacked, elane_ref[...])
    dtb, dteb, ecb = spread[0:length], spread[length:2 * length], spread[2 * length:3 * length]
    cdb = spread[3 * length:3 * length + 1]
    cum_col = _dot3_rhs01(cum, ewide_ref[...])
    cum_row = _dot3(erow_ref[...], cum, (((1,), (1,)), ((), ())))

    xdt = xs_ref[...] * dtb
    xdtb = xdt.astype(BF16)
    xw = (xdt * dteb).astype(BF16)
    bmb, cmb = b_ref[...].astype(BF16), c_ref[...].astype(BF16)
    causal = trif_ref[...] > 0.5
    for g in range(ngs):
        bg, cg = bmb[:, g * nst:(g + 1) * nst], cmb[:, g * nst:(g + 1) * nst]
        cb = lax.dot_general(cg, bg, (((1,), (1,)), ((), ())), preferred_element_type=F32)
        ys = []
        for hh in range(hpg):
            hidx = g * hpg + hh
            seg = cum_col[:, hidx * length:(hidx + 1) * length] - cum_row[g * SUBLANES + hh:g * SUBLANES + hh + 1, :]
            dec = jnp.exp(jnp.where(causal, seg, -jnp.inf))
            m = (cb * dec).astype(BF16)
            ys.append(jnp.dot(m, xdtb[:, hidx * p:(hidx + 1) * p], preferred_element_type=F32))
        gsl = slice(g * gw, (g + 1) * gw)
        sgt = sgt_ref[g]
        y_off = jnp.dot(cg, sgt.astype(BF16), preferred_element_type=F32) * ecb[:, gsl]
        y_ref[:, gsl] = jnp.concatenate(ys, axis=1) + y_off
        upd = lax.dot_general(bg, xw[:, gsl], (((0,), (0,)), ((), ())), preferred_element_type=F32)
        sgt_new = sgt * cdb[:, gsl] + upd
        sgt_ref[g] = sgt_new
        sfin_ref[g] = sgt_new


def _ssd_scan(xbc, dt, dt_bias, a_row, s0, n_heads, direction, stream_name):
    n = xbc.shape[0]
    w = n_heads * SSD_HEAD_DIM
    hpg = n_heads // SSD_GROUPS
    gw = hpg * SSD_HEAD_DIM
    length = SSD_CHUNK
    nch = n // length
    rev = direction == 1
    e_lane, e_wide, e_row = _ssd_expanders(direction, n_heads)
    tri01, trif = _ssd_consts(rev)
    ngs = SSD_GROUPS_PER_STEP
    nb0 = w // (ngs * SSD_STATE)
    nbg = SSD_GROUPS // ngs

    def ch(j):
        return (nch - 1 - j) if rev else j

    const = lambda shape: pl.BlockSpec(shape, lambda g, j: (0,) * len(shape))
    return pl.pallas_call(
        functools.partial(_ssd_kernel, rev=rev, hpg=hpg, ngs=ngs),
        out_shape=(jax.ShapeDtypeStruct((n, w), F32),
                   jax.ShapeDtypeStruct((SSD_GROUPS, SSD_STATE, gw), F32)),
        grid=(nbg, nch),
        in_specs=[pl.BlockSpec((length, ngs * gw), lambda g, j: (ch(j), g)),
                  pl.BlockSpec((length, ngs * SSD_STATE), lambda g, j: (ch(j), nb0 + g)),
                  pl.BlockSpec((length, ngs * SSD_STATE), lambda g, j: (ch(j), nb0 + nbg + g)),
                  pl.BlockSpec((length, LANES), lambda g, j: (ch(j), 0)),
                  const((1, LANES)), const((1, LANES)),
                  pl.BlockSpec((LANES, ngs * gw), lambda g, j: (0, g)),
                  pl.BlockSpec((LANES, ngs * hpg * length), lambda g, j: (0, g)),
                  pl.BlockSpec((ngs * SUBLANES, LANES), lambda g, j: (g, 0)),
                  const((length, length)), const((length, length)),
                  pl.BlockSpec((ngs, SSD_STATE, gw), lambda g, j: (g, 0, 0))],
        out_specs=(pl.BlockSpec((length, ngs * gw), lambda g, j: (ch(j), g)),
                   pl.BlockSpec((ngs, SSD_STATE, gw), lambda g, j: (g, 0, 0))),
        scratch_shapes=[pltpu.VMEM((ngs, SSD_STATE, gw), F32)],
        compiler_params=pltpu.CompilerParams(dimension_semantics=("arbitrary", "arbitrary"),
                                             vmem_limit_bytes=_vmem_limit(24 * 1024 * 1024)),
        name="ssd_scan_%s_%s" % (stream_name, "bw" if rev else "fw"),
    )(xbc, xbc, xbc, dt, dt_bias, a_row, e_lane, e_wide, e_row, tri01, trif, s0)


def _readout_kernel(of_ref, ob_ref, g_ref, hw_ref, yx_ref, ycf_ref, ycb_ref, xsc_ref, z_ref, d_ref, sw_ref,
                    o_ref, *, n_xblk, hg_w, ssd_group):
    i = pl.program_id(0)
    o = of_ref[...] + ob_ref[...]
    gate = _silu(g_ref[...])
    hw = hw_ref[...]
    for h in range(hg_w // HG_HEAD_DIM):
        sl = slice(h * HG_HEAD_DIM, (h + 1) * HG_HEAD_DIM)
        oh = o[:, sl]
        ms = jnp.mean(oh * oh, axis=-1, keepdims=True)
        o_ref[:, sl] = (oh * lax.rsqrt(ms + EPS) * hw[:, sl] * gate[:, sl]).astype(o_ref.dtype)

    y_c = ycf_ref[...] + ycb_ref[...] + d_ref[...] * xsc_ref[...]
    y = jnp.where(i < n_xblk, yx_ref[...], y_c) * _silu(z_ref[...])
    sw = sw_ref[...]
    for g in range(y.shape[1] // ssd_group):
        sl = slice(g * ssd_group, (g + 1) * ssd_group)
        yg = y[:, sl]
        ms = jnp.mean(yg * yg, axis=-1, keepdims=True)
        o_ref[:, hg_w + g * ssd_group:hg_w + (g + 1) * ssd_group] = (
            yg * lax.rsqrt(ms + EPS) * sw[:, sl]).astype(o_ref.dtype)


def _readout(o_fw, o_bw, p_hg, hg_norm_w, y_x, yc_fw, yc_bw, xbc_c, p_z, d_row, ssd_norm_w, n_x):
    t_all, hg_w = o_fw.shape
    sw_w = p_z.shape[1]
    n_c = t_all - n_x
    tm = n_c
    n_xblk = n_x // tm
    xi = lambda i: (jnp.minimum(i, n_xblk - 1), 0)
    ci = lambda i: (0, 0)
    need = 2 * tm * 4 * (3 * hg_w + 5 * sw_w) + 2 * tm * (hg_w + sw_w) * 2 + 4 * tm * (hg_w + sw_w) * 4
    return pl.pallas_call(
        functools.partial(_readout_kernel, n_xblk=n_xblk, hg_w=hg_w, ssd_group=sw_w // SSD_GROUPS),
        out_shape=jax.ShapeDtypeStruct((t_all, hg_w + sw_w), BF16),
        grid=(t_all // tm,),
        in_specs=[pl.BlockSpec((tm, hg_w), lambda i: (i, 0)),
                  pl.BlockSpec((tm, hg_w), lambda i: (i, 0)),
                  pl.BlockSpec((tm, hg_w), lambda i: (i, 4)),
                  pl.BlockSpec((1, hg_w), lambda i: (0, 0)),
                  pl.BlockSpec((tm, sw_w), xi),
                  pl.BlockSpec((tm, sw_w), ci), pl.BlockSpec((tm, sw_w), ci), pl.BlockSpec((tm, sw_w), ci),
                  pl.BlockSpec((tm, sw_w), lambda i: (i, 0)),
                  pl.BlockSpec((1, sw_w), lambda i: (0, 0)),
                  pl.BlockSpec((1, sw_w), lambda i: (0, 0))],
        out_specs=pl.BlockSpec((tm, hg_w + sw_w), lambda i: (i, 0)),
        compiler_params=pltpu.CompilerParams(dimension_semantics=("arbitrary",),
                                             vmem_limit_bytes=_vmem_limit(need)),
        name="mixer_readout",
    )(o_fw, o_bw, p_hg, hg_norm_w, y_x, yc_fw, yc_bw, xbc_c, p_z, d_row, ssd_norm_w)


def _norm_route_kernel(x_ref, nw_ref, sh_ref, sc_ref, w_ref, b_ref, pk_ref, id_ref, wt_ref, *, n_x, tm):
    i = pl.program_id(0)
    x = x_ref[...]
    ms = jnp.mean(x * x, axis=-1, keepdims=True)
    f = x * lax.rsqrt(ms + EPS) * nw_ref[...]
    f = f * (1.0 + _row_select(i, tm, n_x, sc_ref)) + _row_select(i, tm, n_x, sh_ref)
    pk_ref[...] = _pack_bf16_pairs(f)

    logits = jnp.dot(f, w_ref[...], preferred_element_type=F32,
                     precision=lax.Precision.HIGHEST) + b_ref[...]
    lane_i = lax.broadcasted_iota(I32, (tm, LANES), 1)
    lane = lane_i.astype(F32)
    neg = -jnp.inf
    big = float(LANES)
    is_g = lane_i < MOE_GROUPS
    gl = jnp.where(is_g, logits, neg)
    gmax = jnp.max(gl, axis=-1, keepdims=True)
    gsum = jnp.sum(jnp.where(is_g, jnp.exp(gl - gmax), 0.0), axis=-1, keepdims=True)
    p_top = 1.0 / gsum
    g_idx = jnp.min(jnp.where(gl == gmax, lane, big), axis=-1, keepdims=True)
    e_lo = MOE_GROUPS + g_idx * EXPERTS_PER_GROUP
    in_grp = (lane >= e_lo) & (lane < e_lo + EXPERTS_PER_GROUP)
    el = jnp.where(in_grp, logits, neg)
    m1 = jnp.max(el, axis=-1, keepdims=True)
    i1 = jnp.min(jnp.where(in_grp & (el == m1), lane, big), axis=-1, keepdims=True)
    el2 = jnp.where(lane == i1, neg, el)
    m2 = jnp.max(el2, axis=-1, keepdims=True)
    i2 = jnp.min(jnp.where(in_grp & (lane != i1) & (el2 == m2), lane, big), axis=-1, keepdims=True)
    e2 = jnp.exp(m2 - m1)
    w1 = p_top / (1.0 + e2)
    w2 = p_top * e2 / (1.0 + e2)
    ids = jnp.where(lane_i == 0, i1 - MOE_GROUPS, jnp.where(lane_i == 1, i2 - MOE_GROUPS, 0.0))
    id_ref[...] = ids.astype(I32)
    wt_ref[...] = jnp.where(lane_i == 0, w1, jnp.where(lane_i == 1, w2, 0.0))


def _norm_route(x, norm_w, shift2, scale2, w_r, b_r, n_rows, n_x):
    d = x.shape[1]
    tm = 256
    row = lambda w: pl.BlockSpec((tm, w), lambda i: (i, 0))
    fixed = lambda r, w: pl.BlockSpec((r, w), lambda i: (0, 0))
    return pl.pallas_call(
        functools.partial(_norm_route_kernel, n_x=n_x, tm=tm),
        out_shape=(jax.ShapeDtypeStruct((n_rows, d // 2), jnp.uint32),
                   jax.ShapeDtypeStruct((n_rows, LANES), I32), jax.ShapeDtypeStruct((n_rows, LANES), F32)),
        grid=(n_rows // tm,),
        in_specs=[row(d), fixed(1, d), fixed(2, d), fixed(2, d), fixed(d, LANES), fixed(1, LANES)],
        out_specs=(row(d // 2), row(LANES), row(LANES)),
        compiler_params=pltpu.CompilerParams(dimension_semantics=("arbitrary",),
                                             vmem_limit_bytes=_vmem_limit(8 * tm * d * 4 + 4 * d * LANES * 4)),
        name="moe_norm_route",
    )(x, norm_w, shift2, scale2, w_r, b_r)


def _expert_kernel(tok_ref, blk_e_ref, n_used_ref, src_ref, wg_ref, wu_ref, wd_ref, o_ref, xbuf_ref, sem_ref,
                   *, rows):
    b = pl.program_id(0)
    n_used = n_used_ref[0]
    slot = lax.rem(b, 2)

    def row_copy(blk, slt, r):
        return pltpu.make_async_copy(src_ref.at[pl.ds(tok_ref[blk * rows + r], 1), :],
                                     xbuf_ref.at[slt, pl.ds(r, 1), :], sem_ref.at[slt])

    def start_block(blk, slt):
        def body(r, carry):
            row_copy(blk, slt, r).start()
            return carry
        lax.fori_loop(0, rows, body, 0, unroll=8)

    def wait_block(blk, slt):
        def body(r, carry):
            row_copy(blk, slt, r).wait()
            return carry
        lax.fori_loop(0, rows, body, 0, unroll=8)

    @pl.when((b == 0) & (n_used > 0))
    def _():
        start_block(0, 0)

    @pl.when(b + 1 < n_used)
    def _():
        start_block(b + 1, 1 - slot)

    @pl.when(b < n_used)
    def _():
        wait_block(b, slot)
        xa, xb = _unpack_bf16_pairs(xbuf_ref[slot])
        half = xa.shape[1]
        xa, xb = xa.astype(BF16), xb.astype(BF16)

        def proj(w_ref):
            return (jnp.dot(xa, w_ref[0:half, :].astype(BF16), preferred_element_type=F32)
                    + jnp.dot(xb, w_ref[half:2 * half, :].astype(BF16), preferred_element_type=F32))

        h = (_silu(proj(wg_ref)) * proj(wu_ref)).astype(BF16)
        y = jnp.dot(h, wd_ref[...].astype(BF16), preferred_element_type=F32)
        o_ref[...] = _pack_bf16_pairs(y)

    @pl.when(b >= n_used)
    def _():
        o_ref[...] = jnp.zeros_like(o_ref)


def _expert_mlp(buf_tok, blk_e, n_used, packed, wg_all, wu_all, wd_all, layer):
    d, hid = wg_all.shape[2], wg_all.shape[3]
    rows = MOE_ROWS
    nr = buf_tok.shape[0]
    w_spec = lambda r, c: pl.BlockSpec((None, None, r, c), lambda b, tok, be, nu: (layer, be[b], 0, 0),
                                       pipeline_mode=pl.Buffered(1))
    need = 3 * d * hid * 4 + 2 * rows * (d // 2) * 4 + 2 * rows * d * 4 + 3 * d * hid * 2 + 6 * rows * hid * 4
    return pl.pallas_call(
        functools.partial(_expert_kernel, rows=rows),
        out_shape=jax.ShapeDtypeStruct((nr, d // 2), jnp.uint32),
        grid_spec=pltpu.PrefetchScalarGridSpec(
            num_scalar_prefetch=3, grid=(nr // rows,),
            in_specs=[pl.BlockSpec(memory_space=pl.ANY), w_spec(d, hid), w_spec(d, hid), w_spec(hid, d)],
            out_specs=pl.BlockSpec((rows, d // 2), lambda b, tok, be, nu: (b, 0)),
            scratch_shapes=[pltpu.VMEM((2, rows, d // 2), jnp.uint32), pltpu.SemaphoreType.DMA((2,))]),
        compiler_params=pltpu.CompilerParams(dimension_semantics=("arbitrary",),
                                             vmem_limit_bytes=_vmem_limit(need)),
        name="moe_expert_mlp",
    )(buf_tok, blk_e, n_used, packed, wg_all, wu_all, wd_all)


def _combine_kernel(d0_ref, d1_ref, yb_ref, x_ref, wt_ref, g_ref, nw_ref, o_ref, buf_ref, sem_ref,
                    *, rows, n_x, final_norm):
    i = pl.program_id(0)
    base = i * rows

    def copies(r):
        return (pltpu.make_async_copy(yb_ref.at[pl.ds(d0_ref[base + r], 1), :],
                                      buf_ref.at[0, pl.ds(r, 1), :], sem_ref.at[0]),
                pltpu.make_async_copy(yb_ref.at[pl.ds(d1_ref[base + r], 1), :],
                                      buf_ref.at[1, pl.ds(r, 1), :], sem_ref.at[1]))

    def start(r, carry):
        for cp in copies(r):
            cp.start()
        return carry

    def wait(r, carry):
        for cp in copies(r):
            cp.wait()
        return carry

    lax.fori_loop(0, rows, start, 0, unroll=8)
    lax.fori_loop(0, rows, wait, 0, unroll=8)
    wt = wt_ref[...]
    w0, w1 = wt[:, 0:1], wt[:, 1:2]
    gate = _row_select(i, rows, n_x, g_ref)
    half = buf_ref.shape[2]
    halves = []
    for part0, part1, sl in zip(_unpack_bf16_pairs(buf_ref[0]), _unpack_bf16_pairs(buf_ref[1]),
                                (slice(0, half), slice(half, 2 * half))):
        halves.append(x_ref[:, sl] + gate[:, sl] * (part0 * w0 + part1 * w1))
    if final_norm:
        ssq = sum(jnp.sum(hv * hv, axis=-1, keepdims=True) for hv in halves)
        scale = lax.rsqrt(ssq / (2 * half) + EPS)
        halves = [hv * scale * nw_ref[:, sl] for hv, sl in zip(halves, (slice(0, half), slice(half, 2 * half)))]
    o_ref[:, 0:half] = halves[0]
    o_ref[:, half:2 * half] = halves[1]


def _combine(dest0, dest1, yb, x_stream, wts, gate2, norm_w, n_rows, n_x, final_norm):
    d = 2 * yb.shape[1]
    rows = MOE_ROWS
    return pl.pallas_call(
        functools.partial(_combine_kernel, rows=rows, n_x=n_x, final_norm=final_norm),
        out_shape=jax.ShapeDtypeStruct((n_rows, d), F32),
        grid_spec=pltpu.PrefetchScalarGridSpec(
            num_scalar_prefetch=2, grid=(n_rows // rows,),
            in_specs=[pl.BlockSpec(memory_space=pl.ANY),
                      pl.BlockSpec((rows, d), lambda i, a, b: (i, 0)),
                      pl.BlockSpec((rows, LANES), lambda i, a, b: (i, 0)),
                      pl.BlockSpec((2, d), lambda i, a, b: (0, 0)),
                      pl.BlockSpec((1, d), lambda i, a, b: (0, 0))],
            out_specs=pl.BlockSpec((rows, d), lambda i, a, b: (i, 0)),
            scratch_shapes=[pltpu.VMEM((2, rows, d // 2), jnp.uint32), pltpu.SemaphoreType.DMA((2,))]),
        compiler_params=pltpu.CompilerParams(dimension_semantics=("arbitrary",),
                                             vmem_limit_bytes=_vmem_limit(6 * rows * d * 4 + 2 * rows * d * 4)),
        name="moe_combine",
    )(dest0, dest1, yb, x_stream, wts, gate2, norm_w)


def _dispatch_tables(eid, n_tok):
    n_assign = n_tok * TOP_K
    flat_e = eid.reshape(-1)
    order = jnp.argsort(flat_e, stable=True)
    se = flat_e[order]
    counts = jnp.bincount(flat_e, length=N_EXPERTS)
    padded = (counts + MOE_ROWS - 1) // MOE_ROWS * MOE_ROWS
    pad_end = jnp.cumsum(padded)
    pad_start = pad_end - padded
    start = jnp.cumsum(counts) - counts
    dest_sorted = (pad_start[se] + jnp.arange(n_assign, dtype=I32) - start[se]).astype(I32)
    n_blocks = -(-n_assign // MOE_ROWS) + N_EXPERTS
    buf_tok = jnp.zeros((n_blocks * MOE_ROWS,), I32).at[dest_sorted].set((order // TOP_K).astype(I32))
    dest = jnp.zeros((n_assign,), I32).at[order].set(dest_sorted).reshape(n_tok, TOP_K)
    blk_e = jnp.minimum(jnp.searchsorted(pad_end, jnp.arange(n_blocks, dtype=I32) * MOE_ROWS, side='right'),
                        N_EXPERTS - 1).astype(I32)
    n_used = (pad_end[-1] // MOE_ROWS).astype(I32).reshape(1)
    return buf_tok, dest, blk_e, n_used, n_blocks


def _hier_moe(stream, n_rows, n_x, norm_w, shift2, scale2, gate2, w_r, b_r, wg_all, wu_all, wd_all, layer, final_w):
    packed, ids, wts = _norm_route(stream, norm_w, shift2, scale2, w_r, b_r, n_rows, n_x)
    buf_tok, dest, blk_e, n_used, _ = _dispatch_tables(ids[:, :TOP_K], n_rows)
    yb = _expert_mlp(buf_tok, blk_e, n_used, packed, wg_all, wu_all, wd_all, layer)
    final_norm = final_w is not None
    nw = final_w if final_norm else norm_w
    return _combine(dest[:, 0], dest[:, 1], yb, stream, wts, gate2, nw, n_rows, n_x, final_norm)


def kernel(x, c, ctx, c_ctx, w_mod, b_mod, norm1_w, w_in, hgrn_lower_bounds, hgrn_norm_w, ssd_conv_w, ssd_conv_b,
           ssd_dt_bias, ssd_a_log, ssd_d, ssd_norm_w, w_out, norm2_w, w_group_router, b_group_router,
           w_expert_router, b_expert_router, w_gate, w_up, w_down, final_norm_w):
    bsz, seq, d = x.shape
    assert bsz == 1
    n_ctx = ctx.shape[1]
    depth = w_mod.shape[0]
    t_all = seq + n_ctx
    hg_w = d // 2
    ssd_w = d - hg_w
    n_ssd_heads = ssd_w // SSD_HEAD_DIM
    hg_cols = 5 * hg_w
    conv_dim = ssd_w + 2 * SSD_GROUPS * SSD_STATE
    assert seq // GRID_W == SSD_CHUNK and n_ctx % SSD_CHUNK == 0 and seq % n_ctx == 0
    gw = n_ssd_heads // SSD_GROUPS * SSD_HEAD_DIM
    dt_pad = LANES - 2 * n_ssd_heads

    stream = jnp.concatenate([x[0], ctx[0]], axis=0)
    c2t = jnp.stack([c[0], c_ctx], axis=1)
    lb_soft = jax.nn.softmax(hgrn_lower_bounds.astype(F32), axis=0)
    lower_bounds = jnp.cumsum(lb_soft, axis=0) - lb_soft[0]
    s_zero = jnp.zeros((SSD_GROUPS, SSD_STATE, gw), F32)

    for l in range(depth):
        last = l == depth - 1
        mod = _mod_vectors(c2t, w_mod, l, b_mod[l][None, :]).reshape(2, MOD_CHUNKS, d)
        chunk = lambda k: mod[:, k, :]

        h = _norm_mod(stream, norm1_w[l][None, :], chunk(0), chunk(1), t_all, seq, BF16)
        p_hg = _matmul(h, w_in, l, 0, hg_cols)
        p_z = _matmul(h, w_in, l, hg_cols, ssd_w)
        p_xbc = _matmul(h, w_in, l, hg_cols + ssd_w, conv_dim)
        w_dt = jnp.pad(w_in[l, :, hg_cols + ssd_w + conv_dim:], ((0, 0), (0, dt_pad)))
        p_dt = _matmul(h, w_dt[None], 0, 0, LANES)

        lb = lower_bounds[l]
        outs = []
        for direction in range(2):
            lbd = lb[direction][None, :]
            outs.append(_hgrn_scan(p_hg, 1 + direction, jnp.log(lbd), jnp.log1p(-lbd), 1.0 - lbd, seq,
                                   rev=direction == 1))
        o_fw, o_bw = outs

        xbc_x, xbc_c = _ssd_conv(p_xbc, ssd_conv_w[l], ssd_conv_b[l][None, :], seq)
        dt_bias = jnp.pad(ssd_dt_bias[l].reshape(1, -1), ((0, 0), (0, dt_pad)))
        a_row = jnp.pad(-jnp.exp(ssd_a_log[l].astype(F32)).reshape(1, -1), ((0, 0), (0, dt_pad)))
        dt_x = _permute_latent(p_dt, seq)
        dt_c = p_dt[seq:]
        ys_x, ys_c = [], []
        for direction in range(2):
            y_c, s_c = _ssd_scan(xbc_c, dt_c, dt_bias, a_row, s_zero, n_ssd_heads, direction, "context")
            y_x, _ = _ssd_scan(xbc_x, dt_x, dt_bias, a_row, s_c, n_ssd_heads, direction, "latent")
            ys_c.append(y_c)
            ys_x.append(y_x)
        d_row = jnp.repeat(ssd_d[l].astype(F32), SSD_HEAD_DIM)[None, :]
        y_x = _unpermute_latent(ys_x[0], ys_x[1], xbc_x, d_row)
        ab = _readout(o_fw, o_bw, p_hg, jnp.tile(hgrn_norm_w[l], hg_w // HG_HEAD_DIM)[None, :],
                      y_x, ys_c[0], ys_c[1], xbc_c, p_z, d_row, ssd_norm_w[l][None, :], seq)
        stream = _matmul_residual(ab, w_out, l, stream, chunk(2), seq)

        w_r = jnp.pad(jnp.concatenate([w_group_router[l], w_expert_router[l]], axis=1),
                      ((0, 0), (0, LANES - MOE_GROUPS - N_EXPERTS)))
        b_r = jnp.pad(jnp.concatenate([b_group_router[l], b_expert_router[l]]),
                      (0, LANES - MOE_GROUPS - N_EXPERTS))[None, :]
        n_rows = seq if last else t_all
        stream = _hier_moe(stream, n_rows, seq, norm2_w[l][None, :], chunk(3), chunk(4), chunk(5), w_r, b_r,
                           w_gate, w_up, w_down, l, final_norm_w[None, :] if last else None)
    return stream[:seq].reshape(bsz, seq, d)
```

```python
import functools

import numpy as np
import jax
import jax.numpy as jnp
from jax import lax
from jax.experimental import pallas as pl
from jax.experimental.pallas import tpu as pltpu

F32 = jnp.float32
BF16 = jnp.bfloat16
I32 = jnp.int32

EPS = 1e-6
GRID_W = 64
MOD_CHUNKS = 6

HG_HEAD_DIM = 128
SSD_HEAD_DIM = 64
SSD_GROUPS = 8
SSD_STATE = 128
MOE_GROUPS = 4
EXPERTS_PER_GROUP = 8
N_EXPERTS = MOE_GROUPS * EXPERTS_PER_GROUP
TOP_K = 2

LANES = 128
SUBLANES = 8
VMEM_BYTES_V7X = 64 * 1024 * 1024
HG_CHUNK = 64
HG_SUB = 16
HG_HEADS_PER_STEP = 8
HG_CHUNKS_PER_STEP = 2
SSD_CHUNK = 128
SSD_GROUPS_PER_STEP = 4
MOE_ROWS = 256


def _vmem_limit(nbytes):
    return int(min(max(nbytes * 3 // 2, 16 * 1024 * 1024), VMEM_BYTES_V7X - 8 * 1024 * 1024))


def _split3(x):
    h = x.astype(BF16)
    r = x - h.astype(F32)
    m = r.astype(BF16)
    lo = (r - m.astype(F32)).astype(BF16)
    return h, m, lo


def _dot3(a01, x, dims=None):
    out = None
    for part in _split3(x):
        if dims is None:
            t = jnp.dot(a01, part, preferred_element_type=F32)
        else:
            t = lax.dot_general(a01, part, dims, preferred_element_type=F32)
        out = t if out is None else out + t
    return out


def _dot3_rhs01(x, b01):
    out = None
    for part in _split3(x):
        t = jnp.dot(part, b01, preferred_element_type=F32)
        out = t if out is None else out + t
    return out


def _silu(x):
    return x * jax.nn.sigmoid(x)


def _pack_bf16_pairs(v):
    half = v.shape[1] // 2
    rb = v.astype(BF16).astype(F32)
    hi = lax.bitcast_convert_type(rb[:, :half], jnp.uint32)
    lo = lax.shift_right_logical(lax.bitcast_convert_type(rb[:, half:], jnp.uint32), jnp.uint32(16))
    return hi | lo


def _unpack_bf16_pairs(pk):
    first = lax.bitcast_convert_type(pk & jnp.uint32(0xFFFF0000), F32)
    second = lax.bitcast_convert_type(lax.shift_left(pk, jnp.uint32(16)), F32)
    return first, second


def _mod_kernel(c_ref, w_ref, b_ref, o_ref, acc_ref):
    k = pl.program_id(1)

    @pl.when(k == 0)
    def _():
        acc_ref[...] = jnp.zeros_like(acc_ref)

    c = c_ref[...]
    s = _silu(c)
    w = w_ref[...]
    tk, tn = w.shape
    for r in range(2):
        prod = w * s[:, r:r + 1]
        acc_ref[r] += jnp.sum(prod.reshape(tk // SUBLANES, SUBLANES, tn), axis=0)

    @pl.when(k == pl.num_programs(1) - 1)
    def _():
        o_ref[...] = jnp.sum(acc_ref[...], axis=1) + b_ref[...]


def _mod_vectors(c2t, w_all, layer, b):
    _, d, n = w_all.shape
    tk, tn = min(1024, d), min(2048, n)
    return pl.pallas_call(
        _mod_kernel,
        out_shape=jax.ShapeDtypeStruct((2, n), F32),
        grid=(n // tn, d // tk),
        in_specs=[pl.BlockSpec((tk, 2), lambda j, k: (k, 0)),
                  pl.BlockSpec((None, tk, tn), lambda j, k: (layer, k, j)),
                  pl.BlockSpec((1, tn), lambda j, k: (0, j))],
        out_specs=pl.BlockSpec((2, tn), lambda j, k: (0, j)),
        scratch_shapes=[pltpu.VMEM((2, SUBLANES, tn), F32)],
        compiler_params=pltpu.CompilerParams(
            dimension_semantics=("arbitrary", "arbitrary"),
            vmem_limit_bytes=_vmem_limit(2 * tk * tn * 4 + 4 * tk * tn)),
        name="adaln_matvec",
    )(c2t, w_all, b)


def _row_select(i, tm, n_x, ref):
    row = i * tm + lax.broadcasted_iota(I32, (tm, 1), 0)
    return jnp.where(row < n_x, ref[0:1, :], ref[1:2, :])


def _norm_mod_kernel(x_ref, w_ref, sh_ref, sc_ref, o_ref, *, n_x, tm):
    i = pl.program_id(0)
    x = x_ref[...]
    ms = jnp.mean(x * x, axis=-1, keepdims=True)
    y = x * lax.rsqrt(ms + EPS) * w_ref[...]
    sc = _row_select(i, tm, n_x, sc_ref)
    sh = _row_select(i, tm, n_x, sh_ref)
    o_ref[...] = (y * (1.0 + sc) + sh).astype(o_ref.dtype)


def _norm_mod(x, w, shift2, scale2, n_rows, n_x, out_dtype):
    d = x.shape[1]
    tm = 256
    return pl.pallas_call(
        functools.partial(_norm_mod_kernel, n_x=n_x, tm=tm),
        out_shape=jax.ShapeDtypeStruct((n_rows, d), out_dtype),
        grid=(n_rows // tm,),
        in_specs=[pl.BlockSpec((tm, d), lambda i: (i, 0)),
                  pl.BlockSpec((1, d), lambda i: (0, 0)),
                  pl.BlockSpec((2, d), lambda i: (0, 0)),
                  pl.BlockSpec((2, d), lambda i: (0, 0))],
        out_specs=pl.BlockSpec((tm, d), lambda i: (i, 0)),
        compiler_params=pltpu.CompilerParams(
            dimension_semantics=("arbitrary",),
            vmem_limit_bytes=_vmem_limit(4 * tm * d * 4 + 4 * tm * d * 4)),
        name="norm_modulate",
    )(x, w, shift2, scale2)


def _mm_nt_kernel(a_ref, wt_ref, o_ref, wb_ref):
    @pl.when(pl.program_id(1) == 0)
    def _():
        wb_ref[...] = wt_ref[...].astype(BF16)

    o_ref[...] = lax.dot_general(a_ref[...], wb_ref[...], (((1,), (1,)), ((), ())), preferred_element_type=F32)


def _mm_res_kernel(a_ref, w_ref, r_ref, g_ref, o_ref, wb_ref, *, n_x, tm):
    i = pl.program_id(1)

    @pl.when(i == 0)
    def _():
        wb_ref[...] = w_ref[...].astype(BF16)

    acc = jnp.dot(a_ref[...], wb_ref[...], preferred_element_type=F32)
    o_ref[...] = r_ref[...] + _row_select(i, tm, n_x, g_ref) * acc


def _mm_tiles(m, k, n):
    tm = 1056 if m % 1056 == 0 else (1024 if m % 1024 == 0 else m)
    tn = 512 if n % 512 == 0 else n
    need = 2 * k * tn * 4 + k * tn * 2 + 2 * tm * k * 2 + 4 * tm * tn * 4
    return tm, tn, need


def _matmul_nt(a, wt_all, layer, col_off, n):
    m, k = a.shape
    tm, tn, need = _mm_tiles(m, k, n)
    off = col_off // tn
    assert off * tn == col_off
    return pl.pallas_call(
        _mm_nt_kernel,
        out_shape=jax.ShapeDtypeStruct((m, n), F32),
        grid=(n // tn, m // tm),
        in_specs=[pl.BlockSpec((tm, k), lambda j, i: (i, 0)),
                  pl.BlockSpec((None, tn, k), lambda j, i: (layer, j + off, 0))],
        out_specs=pl.BlockSpec((tm, tn), lambda j, i: (i, j)),
        scratch_shapes=[pltpu.VMEM((tn, k), BF16)],
        compiler_params=pltpu.CompilerParams(
            dimension_semantics=("arbitrary", "arbitrary"), vmem_limit_bytes=_vmem_limit(need)),
        name="proj_matmul",
    )(a, wt_all)


def _matmul_residual(a, w_all, layer, res, gate2, n_x):
    m, k = a.shape
    n = w_all.shape[2]
    tm, tn, need = _mm_tiles(m, k, n)
    return pl.pallas_call(
        functools.partial(_mm_res_kernel, n_x=n_x, tm=tm),
        out_shape=jax.ShapeDtypeStruct((m, n), F32),
        grid=(n // tn, m // tm),
        in_specs=[pl.BlockSpec((tm, k), lambda j, i: (i, 0)),
                  pl.BlockSpec((None, k, tn), lambda j, i: (layer, 0, j)),
                  pl.BlockSpec((tm, tn), lambda j, i: (i, j)),
                  pl.BlockSpec((2, tn), lambda j, i: (0, j))],
        out_specs=pl.BlockSpec((tm, tn), lambda j, i: (i, j)),
        scratch_shapes=[pltpu.VMEM((k, tn), BF16)],
        compiler_params=pltpu.CompilerParams(
            dimension_semantics=("arbitrary", "arbitrary"), vmem_limit_bytes=_vmem_limit(need)),
        name="out_proj_residual",
    )(a, w_all, res, gate2)


def _hgrn_consts(rev):
    c, sub = HG_CHUNK, HG_SUB
    t = np.arange(c)
    if not rev:
        tri = t[None, :] <= t[:, None]
        mid = (t // sub) * sub + sub // 2 - 1
        mmid = t[None, :] <= mid[:, None]
    else:
        tri = t[None, :] >= t[:, None]
        mid = (t // sub) * sub + sub // 2
        mmid = t[None, :] >= mid[:, None]
    mcat = np.concatenate([tri, mmid, np.ones((c, c), bool)], axis=0).astype(np.float32)
    return jnp.asarray(mcat, BF16), jnp.asarray(tri.astype(np.float32), F32)


def _hgrn_kernel(q_ref, f_ref, v_ref, loglb_ref, l1mlb_ref, omlb_ref, mcat_ref, tri_ref,
                 o_ref, st_ref, *, rev, heads):
    hg = pl.program_id(0)
    j = pl.program_id(1)
    c, sub, dk = HG_CHUNK, HG_SUB, HG_HEAD_DIM
    nsub = c // sub

    @pl.when(j == 0)
    def _():
        st_ref[...] = jnp.zeros_like(st_ref)

    s = f_ref[...]
    e = jnp.exp(-jnp.abs(s))
    lse = jnp.log1p(e)
    log_sig = jnp.minimum(s, 0.0) - lse
    a = loglb_ref[...]
    cc = l1mlb_ref[...] + log_sig
    logf = jnp.maximum(a, cc) + jnp.log1p(jnp.exp(-jnp.abs(a - cc)))
    sig_neg = jnp.where(s >= 0.0, e, 1.0) / (1.0 + e)
    kk_all = omlb_ref[...] * sig_neg
    q_all = q_ref[...]
    vb_all = v_ref[...].astype(BF16)
    causal = tri_ref[...] > 0.5
    zero_tile = jnp.zeros((sub, dk), BF16)
    nck = q_all.shape[0] // c

    for ck in (range(nck - 1, -1, -1) if rev else range(nck)):
        rs = slice(ck * c, (ck + 1) * c)
        q, kk, vb = q_all[rs], kk_all[rs], vb_all[rs]
        ball3 = _dot3(mcat_ref[...], logf[rs])
        b, bmid, btot = ball3[0:c], ball3[c:2 * c], ball3[2 * c:3 * c]
        ka = (kk * jnp.exp(bmid - b)).astype(BF16)
        qs = (q * jnp.exp(b)).astype(BF16)
        ks = (kk * jnp.exp(btot - b)).astype(BF16)
        etot = jnp.exp(btot[0:1, :])

        for h in range(heads):
            sl = slice(h * dk, (h + 1) * dk)
            bh, qh, bmh, kah = b[:, sl], q[:, sl], bmid[:, sl], ka[:, sl]
            lhs_parts, rhs_rows = [], []
            for jb in range(nsub):
                r0, r1 = (jb * sub, c) if not rev else (0, (jb + 1) * sub)
                ref_row = bmh[jb * sub:jb * sub + 1, :]
                part = (qh[r0:r1] * jnp.exp(bh[r0:r1] - ref_row)).astype(BF16)
                pieces = []
                if r0 > 0:
                    pieces.append(jnp.zeros((r0, dk), BF16))
                pieces.append(part)
                if r1 < c:
                    pieces.append(jnp.zeros((c - r1, dk), BF16))
                lhs_parts.append(jnp.concatenate(pieces, axis=0) if len(pieces) > 1 else part)
                row = [zero_tile] * nsub
                row[jb] = kah[jb * sub:(jb + 1) * sub]
                rhs_rows.append(jnp.concatenate(row, axis=1))
            lhs = jnp.concatenate(lhs_parts, axis=1)
            rhs = jnp.concatenate(rhs_rows, axis=0)
            scores = lax.dot_general(lhs, rhs, (((1,), (1,)), ((), ())), preferred_element_type=F32)
            amat = jnp.where(causal, scores, 0.0).astype(BF16)
            hidx = hg * heads + h
            st = st_ref[hidx]
            out = jnp.dot(amat, vb[:, sl], preferred_element_type=F32)
            out = out + lax.dot_general(qs[:, sl], st.astype(BF16), (((1,), (1,)), ((), ())),
                                        preferred_element_type=F32)
            o_ref[rs, sl] = out
            upd = lax.dot_general(vb[:, sl], ks[:, sl], (((0,), (0,)), ((), ())), preferred_element_type=F32)
            st_ref[hidx] = st * etot[:, sl] + upd


def _hgrn_scan(p_hg, f_seg, loglb, l1mlb, omlb, n_x, rev):
    t_all, w = p_hg.shape[0], p_hg.shape[1] // 5
    c = HG_CHUNK
    heads = HG_HEADS_PER_STEP
    bw = heads * HG_HEAD_DIM
    nseg = w // bw
    n_heads = w // HG_HEAD_DIM
    rows = HG_CHUNKS_PER_STEP * c
    ncx, nct = n_x // rows, t_all // rows
    ncc = nct - ncx
    mcat, tri = _hgrn_consts(rev)

    def chunk(j):
        if rev:
            return jnp.where(j < ncc, nct - 1 - j, ncx - 1 - (j - ncc))
        return jnp.where(j < ncc, ncx + j, j - ncc)

    def seg_spec(seg):
        return pl.BlockSpec((rows, bw), lambda hg, j: (chunk(j), seg * nseg + hg))

    vec_spec = pl.BlockSpec((1, bw), lambda hg, j: (0, hg))
    return pl.pallas_call(
        functools.partial(_hgrn_kernel, rev=rev, heads=heads),
        out_shape=jax.ShapeDtypeStruct((t_all, w), F32),
        grid=(nseg, nct),
        in_specs=[seg_spec(0), seg_spec(f_seg), seg_spec(3), vec_spec, vec_spec, vec_spec,
                  pl.BlockSpec((3 * c, c), lambda hg, j: (0, 0)),
                  pl.BlockSpec((c, c), lambda hg, j: (0, 0))],
        out_specs=pl.BlockSpec((rows, bw), lambda hg, j: (chunk(j), hg)),
        scratch_shapes=[pltpu.VMEM((n_heads, HG_HEAD_DIM, HG_HEAD_DIM), F32)],
        compiler_params=pltpu.CompilerParams(
            dimension_semantics=("arbitrary", "arbitrary"),
            vmem_limit_bytes=_vmem_limit(24 * rows * bw * 4 + n_heads * HG_HEAD_DIM * HG_HEAD_DIM * 4)),
        name="hgrn2_scan_bw" if rev else "hgrn2_scan_fw",
    )(p_hg, p_hg, p_hg, loglb, l1mlb, omlb, mcat, tri)


def _shift_rows(u, down):
    n = u.shape[0]
    row = lax.broadcasted_iota(I32, u.shape, 0)
    if down:
        return jnp.where(row == 0, 0.0, pltpu.roll(u, 1, 0))
    return jnp.where(row == n - 1, 0.0, pltpu.roll(u, n - 1, 0))


def _to_scan_order(src_ref, dst_ref, rows):
    for col in range(GRID_W):
        dst_ref[col * rows:(col + 1) * rows, :] = src_ref[pl.ds(col, rows, stride=GRID_W), :]


def _conv_x_kernel(u_ref, w_ref, b_ref, o_ref, y_ref, *, rows):
    gw = GRID_W
    w0, w1, w2, bias = w_ref[0:1, :], w_ref[1:2, :], w_ref[2:3, :], b_ref[...]

    def slab(r):
        start = r * gw if isinstance(r, int) else pl.multiple_of(r * gw, gw)
        return pl.ds(start, gw)

    def emit(r, up, dn):
        y = w0 * up + w1 * u_ref[slab(r), :] + w2 * dn + bias
        y_ref[slab(r), :] = _silu(y)

    emit(0, _shift_rows(u_ref[slab(rows - 1), :], True), u_ref[slab(1), :])
    emit(rows - 1, u_ref[slab(rows - 2), :], _shift_rows(u_ref[slab(0), :], False))

    def body(r, carry):
        emit(r, u_ref[slab(r - 1), :], u_ref[slab(r + 1), :])
        return carry

    lax.fori_loop(1, rows - 1, body, 0)
    _to_scan_order(y_ref, o_ref, rows)


def _conv_c_kernel(u_ref, w_ref, b_ref, o_ref):
    u = u_ref[...]
    y = w_ref[0:1, :] * _shift_rows(u, True) + w_ref[1:2, :] * u + w_ref[2:3, :] * _shift_rows(u, False)
    o_ref[...] = _silu(y + b_ref[...])


def _permute_kernel(u_ref, o_ref, *, rows):
    _to_scan_order(u_ref, o_ref, rows)


def _unpermute_kernel(yf_ref, yb_ref, xs_ref, d_ref, o_ref, *, rows):
    d = d_ref[...]
    for col in range(GRID_W):
        sl = slice(col * rows, (col + 1) * rows)
        o_ref[pl.ds(col, rows, stride=GRID_W), :] = yf_ref[sl, :] + yb_ref[sl, :] + d * xs_ref[sl, :]


def _ssd_conv(p_xbc, conv_w, conv_b, n_x):
    t_all, f = p_xbc.shape
    rows = n_x // GRID_W
    fb = LANES
    blk = n_x * fb * 4
    xbc_x = pl.pallas_call(
        functools.partial(_conv_x_kernel, rows=rows),
        out_shape=jax.ShapeDtypeStruct((n_x, f), F32),
        grid=(f // fb,),
        in_specs=[pl.BlockSpec((n_x, fb), lambda k: (0, k)),
                  pl.BlockSpec((3, fb), lambda k: (0, k)),
                  pl.BlockSpec((1, fb), lambda k: (0, k))],
        out_specs=pl.BlockSpec((n_x, fb), lambda k: (0, k)),
        scratch_shapes=[pltpu.VMEM((n_x, fb), F32)],
        compiler_params=pltpu.CompilerParams(
            dimension_semantics=("arbitrary",), vmem_limit_bytes=_vmem_limit(5 * blk)),
        name="ssd_conv_latent",
    )(p_xbc, conv_w, conv_b)
    n_c = t_all - n_x
    fc = 1024
    xbc_c = pl.pallas_call(
        _conv_c_kernel,
        out_shape=jax.ShapeDtypeStruct((n_c, f), F32),
        grid=(f // fc,),
        in_specs=[pl.BlockSpec((n_c, fc), lambda k: (n_x // n_c, k)),
                  pl.BlockSpec((3, fc), lambda k: (0, k)),
                  pl.BlockSpec((1, fc), lambda k: (0, k))],
        out_specs=pl.BlockSpec((n_c, fc), lambda k: (0, k)),
        compiler_params=pltpu.CompilerParams(
            dimension_semantics=("arbitrary",), vmem_limit_bytes=_vmem_limit(4 * n_c * fc * 4)),
        name="ssd_conv_context",
    )(p_xbc, conv_w, conv_b)
    return xbc_x, xbc_c


def _permute_latent(u, n_x):
    rows = n_x // GRID_W
    return pl.pallas_call(
        functools.partial(_permute_kernel, rows=rows),
        out_shape=jax.ShapeDtypeStruct((n_x, u.shape[1]), F32),
        grid=(1,),
        in_specs=[pl.BlockSpec((n_x, u.shape[1]), lambda k: (0, 0))],
        out_specs=pl.BlockSpec((n_x, u.shape[1]), lambda k: (0, 0)),
        compiler_params=pltpu.CompilerParams(
            dimension_semantics=("arbitrary",), vmem_limit_bytes=_vmem_limit(4 * n_x * u.shape[1] * 4)),
        name="ssd_dt_to_scan_order",
    )(u)


def _unpermute_latent(y_fw, y_bw, xbc_x, d_row):
    n_x, w = y_fw.shape
    rows = n_x // GRID_W
    fb = LANES
    return pl.pallas_call(
        functools.partial(_unpermute_kernel, rows=rows),
        out_shape=jax.ShapeDtypeStruct((n_x, w), F32),
        grid=(w // fb,),
        in_specs=[pl.BlockSpec((n_x, fb), lambda k: (0, k)),
                  pl.BlockSpec((n_x, fb), lambda k: (0, k)),
                  pl.BlockSpec((n_x, fb), lambda k: (0, k)),
                  pl.BlockSpec((1, fb), lambda k: (0, k))],
        out_specs=pl.BlockSpec((n_x, fb), lambda k: (0, k)),
        compiler_params=pltpu.CompilerParams(
            dimension_semantics=("arbitrary",), vmem_limit_bytes=_vmem_limit(8 * n_x * fb * 4)),
        name="ssd_to_natural_order",
    )(y_fw, y_bw, xbc_x, d_row)


def _ssd_consts(rev):
    t = np.arange(SSD_CHUNK)
    tri = (t[None, :] >= t[:, None]) if rev else (t[None, :] <= t[:, None])
    return jnp.asarray(tri.astype(np.float32), BF16), jnp.asarray(tri.astype(np.float32), F32)


def _ssd_expanders(direction, n_heads):
    hpg = n_heads // SSD_GROUPS
    length = SSD_CHUNK
    e_lane = np.zeros((LANES, SSD_GROUPS * hpg * SSD_HEAD_DIM), np.float32)
    e_wide = np.zeros((LANES, SSD_GROUPS * hpg * length), np.float32)
    e_row = np.zeros((SSD_GROUPS * SUBLANES, LANES), np.float32)
    for g in range(SSD_GROUPS):
        for hh in range(hpg):
            col = direction * n_heads + g * hpg + hh
            e_lane[col, (g * hpg + hh) * SSD_HEAD_DIM:(g * hpg + hh + 1) * SSD_HEAD_DIM] = 1.0
            e_wide[col, (g * hpg + hh) * length:(g * hpg + hh + 1) * length] = 1.0
            e_row[g * SUBLANES + hh, col] = 1.0
    return jnp.asarray(e_lane, BF16), jnp.asarray(e_wide, BF16), jnp.asarray(e_row, BF16)


def _ssd_kernel(xs_ref, b_ref, c_ref, dt_ref, dtb_ref, a_ref, elane_ref, ewide_ref, erow_ref,
                tri01_ref, trif_ref, s0_ref, y_ref, sfin_ref, sgt_ref, *, rev, hpg, ngs):
    @pl.when(pl.program_id(1) == 0)
    def _():
        sgt_ref[...] = s0_ref[...]

    length = xs_ref.shape[0]
    p = SSD_HEAD_DIM
    gw = hpg * p
    nst = SSD_STATE
    dt_all = jax.nn.softplus(dt_ref[...] + dtb_ref[...])
    da_all = dt_all * a_ref[...]
    cum = _dot3(tri01_ref[...], da_all)
    tot = cum[0:1, :] if rev else cum[length - 1:length, :]
    dte = jnp.exp(tot - cum)
    ec = jnp.exp(cum)
    etot = jnp.broadcast_to(jnp.exp(tot), (SUBLANES, LANES))
    stacked = jnp.concatenate([dt_all, dte, ec, etot], axis=0)
    spread = _dot3_rhs01(stacked, elane_ref[...])
    dtb, dteb, ecb = spread[0:length], spread[length:2 * length], spread[2 * length:3 * length]
    cdb = spread[3 * length:3 * length + 1]
    cum_col = _dot3_rhs01(cum, ewide_ref[...])
    cum_row = _dot3(erow_ref[...], cum, (((1,), (1,)), ((), ())))

    xdt = xs_ref[...] * dtb
    xdtb = xdt.astype(BF16)
    xw = (xdt * dteb).astype(BF16)
    bmb, cmb = b_ref[...].astype(BF16), c_ref[...].astype(BF16)
    causal = trif_ref[...] > 0.5
    for g in range(ngs):
        bg, cg = bmb[:, g * nst:(g + 1) * nst], cmb[:, g * nst:(g + 1) * nst]
        cb = lax.dot_general(cg, bg, (((1,), (1,)), ((), ())), preferred_element_type=F32)
        ys = []
        for hh in range(hpg):
            hidx = g * hpg + hh
            seg = cum_col[:, hidx * length:(hidx + 1) * length] - cum_row[g * SUBLANES + hh:g * SUBLANES + hh + 1, :]
            dec = jnp.exp(jnp.where(causal, seg, -jnp.inf))
            m = (cb * dec).astype(BF16)
            ys.append(jnp.dot(m, xdtb[:, hidx * p:(hidx + 1) * p], preferred_element_type=F32))
        gsl = slice(g * gw, (g + 1) * gw)
        sgt = sgt_ref[g]
        y_off = jnp.dot(cg, sgt.astype(BF16), preferred_element_type=F32) * ecb[:, gsl]
        y_ref[:, gsl] = jnp.concatenate(ys, axis=1) + y_off
        upd = lax.dot_general(bg, xw[:, gsl], (((0,), (0,)), ((), ())), preferred_element_type=F32)
        sgt_new = sgt * cdb[:, gsl] + upd
        sgt_ref[g] = sgt_new
        sfin_ref[g] = sgt_new


def _ssd_scan(xbc, dt, dt_bias, a_row, s0, n_heads, direction, stream_name):
    n = xbc.shape[0]
    w = n_heads * SSD_HEAD_DIM
    hpg = n_heads // SSD_GROUPS
    gw = hpg * SSD_HEAD_DIM
    length = SSD_CHUNK
    nch = n // length
    rev = direction == 1
    e_lane, e_wide, e_row = _ssd_expanders(direction, n_heads)
    tri01, trif = _ssd_consts(rev)
    ngs = SSD_GROUPS_PER_STEP
    nb0 = w // (ngs * SSD_STATE)
    nbg = SSD_GROUPS // ngs

    def ch(j):
        return (nch - 1 - j) if rev else j

    const = lambda shape: pl.BlockSpec(shape, lambda g, j: (0,) * len(shape))
    return pl.pallas_call(
        functools.partial(_ssd_kernel, rev=rev, hpg=hpg, ngs=ngs),
        out_shape=(jax.ShapeDtypeStruct((n, w), F32),
                   jax.ShapeDtypeStruct((SSD_GROUPS, SSD_STATE, gw), F32)),
        grid=(nbg, nch),
        in_specs=[pl.BlockSpec((length, ngs * gw), lambda g, j: (ch(j), g)),
                  pl.BlockSpec((length, ngs * SSD_STATE), lambda g, j: (ch(j), nb0 + g)),
                  pl.BlockSpec((length, ngs * SSD_STATE), lambda g, j: (ch(j), nb0 + nbg + g)),
                  pl.BlockSpec((length, LANES), lambda g, j: (ch(j), 0)),
                  const((1, LANES)), const((1, LANES)),
                  pl.BlockSpec((LANES, ngs * gw), lambda g, j: (0, g)),
                  pl.BlockSpec((LANES, ngs * hpg * length), lambda g, j: (0, g)),
                  pl.BlockSpec((ngs * SUBLANES, LANES), lambda g, j: (g, 0)),
                  const((length, length)), const((length, length)),
                  pl.BlockSpec((ngs, SSD_STATE, gw), lambda g, j: (g, 0, 0))],
        out_specs=(pl.BlockSpec((length, ngs * gw), lambda g, j: (ch(j), g)),
                   pl.BlockSpec((ngs, SSD_STATE, gw), lambda g, j: (g, 0, 0))),
        scratch_shapes=[pltpu.VMEM((ngs, SSD_STATE, gw), F32)],
        compiler_params=pltpu.CompilerParams(dimension_semantics=("arbitrary", "arbitrary"),
                                             vmem_limit_bytes=_vmem_limit(24 * 1024 * 1024)),
        name="ssd_scan_%s_%s" % (stream_name, "bw" if rev else "fw"),
    )(xbc, xbc, xbc, dt, dt_bias, a_row, e_lane, e_wide, e_row, tri01, trif, s0)


def _readout_kernel(of_ref, ob_ref, g_ref, hw_ref, yx_ref, ycf_ref, ycb_ref, xsc_ref, z_ref, d_ref, sw_ref,
                    o_ref, *, n_xblk, hg_w, ssd_group):
    i = pl.program_id(0)
    o = of_ref[...] + ob_ref[...]
    gate = _silu(g_ref[...])
    hw = hw_ref[...]
    for h in range(hg_w // HG_HEAD_DIM):
        sl = slice(h * HG_HEAD_DIM, (h + 1) * HG_HEAD_DIM)
        oh = o[:, sl]
        ms = jnp.mean(oh * oh, axis=-1, keepdims=True)
        o_ref[:, sl] = (oh * lax.rsqrt(ms + EPS) * hw[:, sl] * gate[:, sl]).astype(o_ref.dtype)

    y_c = ycf_ref[...] + ycb_ref[...] + d_ref[...] * xsc_ref[...]
    y = jnp.where(i < n_xblk, yx_ref[...], y_c) * _silu(z_ref[...])
    sw = sw_ref[...]
    for g in range(y.shape[1] // ssd_group):
        sl = slice(g * ssd_group, (g + 1) * ssd_group)
        yg = y[:, sl]
        ms = jnp.mean(yg * yg, axis=-1, keepdims=True)
        o_ref[:, hg_w + g * ssd_group:hg_w + (g + 1) * ssd_group] = (
            yg * lax.rsqrt(ms + EPS) * sw[:, sl]).astype(o_ref.dtype)


def _readout(o_fw, o_bw, p_hg, hg_norm_w, y_x, yc_fw, yc_bw, xbc_c, p_z, d_row, ssd_norm_w, n_x):
    t_all, hg_w = o_fw.shape
    sw_w = p_z.shape[1]
    n_c = t_all - n_x
    tm = n_c
    n_xblk = n_x // tm
    xi = lambda i: (jnp.minimum(i, n_xblk - 1), 0)
    ci = lambda i: (0, 0)
    need = 2 * tm * 4 * (3 * hg_w + 5 * sw_w) + 2 * tm * (hg_w + sw_w) * 2 + 4 * tm * (hg_w + sw_w) * 4
    return pl.pallas_call(
        functools.partial(_readout_kernel, n_xblk=n_xblk, hg_w=hg_w, ssd_group=sw_w // SSD_GROUPS),
        out_shape=jax.ShapeDtypeStruct((t_all, hg_w + sw_w), BF16),
        grid=(t_all // tm,),
        in_specs=[pl.BlockSpec((tm, hg_w), lambda i: (i, 0)),
                  pl.BlockSpec((tm, hg_w), lambda i: (i, 0)),
                  pl.BlockSpec((tm, hg_w), lambda i: (i, 4)),
                  pl.BlockSpec((1, hg_w), lambda i: (0, 0)),
                  pl.BlockSpec((tm, sw_w), xi),
                  pl.BlockSpec((tm, sw_w), ci), pl.BlockSpec((tm, sw_w), ci), pl.BlockSpec((tm, sw_w), ci),
                  pl.BlockSpec((tm, sw_w), lambda i: (i, 0)),
                  pl.BlockSpec((1, sw_w), lambda i: (0, 0)),
                  pl.BlockSpec((1, sw_w), lambda i: (0, 0))],
        out_specs=pl.BlockSpec((tm, hg_w + sw_w), lambda i: (i, 0)),
        compiler_params=pltpu.CompilerParams(dimension_semantics=("arbitrary",),
                                             vmem_limit_bytes=_vmem_limit(need)),
        name="mixer_readout",
    )(o_fw, o_bw, p_hg, hg_norm_w, y_x, yc_fw, yc_bw, xbc_c, p_z, d_row, ssd_norm_w)


def _norm_route_kernel(x_ref, nw_ref, sh_ref, sc_ref, w_ref, b_ref, pk_ref, id_ref, wt_ref, *, n_x, tm):
    i = pl.program_id(0)
    x = x_ref[...]
    ms = jnp.mean(x * x, axis=-1, keepdims=True)
    f = x * lax.rsqrt(ms + EPS) * nw_ref[...]
    f = f * (1.0 + _row_select(i, tm, n_x, sc_ref)) + _row_select(i, tm, n_x, sh_ref)
    pk_ref[...] = _pack_bf16_pairs(f)

    logits = jnp.dot(f, w_ref[...], preferred_element_type=F32,
                     precision=lax.Precision.HIGHEST) + b_ref[...]
    lane_i = lax.broadcasted_iota(I32, (tm, LANES), 1)
    lane = lane_i.astype(F32)
    neg = -jnp.inf
    big = float(LANES)
    is_g = lane_i < MOE_GROUPS
    gl = jnp.where(is_g, logits, neg)
    gmax = jnp.max(gl, axis=-1, keepdims=True)
    gsum = jnp.sum(jnp.where(is_g, jnp.exp(gl - gmax), 0.0), axis=-1, keepdims=True)
    p_top = 1.0 / gsum
    g_idx = jnp.min(jnp.where(gl == gmax, lane, big), axis=-1, keepdims=True)
    e_lo = MOE_GROUPS + g_idx * EXPERTS_PER_GROUP
    in_grp = (lane >= e_lo) & (lane < e_lo + EXPERTS_PER_GROUP)
    el = jnp.where(in_grp, logits, neg)
    m1 = jnp.max(el, axis=-1, keepdims=True)
    i1 = jnp.min(jnp.where(in_grp & (el == m1), lane, big), axis=-1, keepdims=True)
    el2 = jnp.where(lane == i1, neg, el)
    m2 = jnp.max(el2, axis=-1, keepdims=True)
    i2 = jnp.min(jnp.where(in_grp & (lane != i1) & (el2 == m2), lane, big), axis=-1, keepdims=True)
    e2 = jnp.exp(m2 - m1)
    w1 = p_top / (1.0 + e2)
    w2 = p_top * e2 / (1.0 + e2)
    ids = jnp.where(lane_i == 0, i1 - MOE_GROUPS, jnp.where(lane_i == 1, i2 - MOE_GROUPS, 0.0))
    id_ref[...] = ids.astype(I32)
    wt_ref[...] = jnp.where(lane_i == 0, w1, jnp.where(lane_i == 1, w2, 0.0))


def _norm_route(x, norm_w, shift2, scale2, w_r, b_r, n_rows, n_x):
    d = x.shape[1]
    tm = 256
    row = lambda w: pl.BlockSpec((tm, w), lambda i: (i, 0))
    fixed = lambda r, w: pl.BlockSpec((r, w), lambda i: (0, 0))
    return pl.pallas_call(
        functools.partial(_norm_route_kernel, n_x=n_x, tm=tm),
        out_shape=(jax.ShapeDtypeStruct((n_rows, d // 2), jnp.uint32),
                   jax.ShapeDtypeStruct((n_rows, LANES), I32), jax.ShapeDtypeStruct((n_rows, LANES), F32)),
        grid=(n_rows // tm,),
        in_specs=[row(d), fixed(1, d), fixed(2, d), fixed(2, d), fixed(d, LANES), fixed(1, LANES)],
        out_specs=(row(d // 2), row(LANES), row(LANES)),
        compiler_params=pltpu.CompilerParams(dimension_semantics=("arbitrary",),
                                             vmem_limit_bytes=_vmem_limit(8 * tm * d * 4 + 4 * d * LANES * 4)),
        name="moe_norm_route",
    )(x, norm_w, shift2, scale2, w_r, b_r)


GATHER_UNROLL = 8


def _expert_kernel(tok_ref, blk_e_ref, nvalid_ref, first_ref, next_e_ref, n_used_ref,
                   src_ref, wg_hbm, wu_hbm, wd_hbm, o_ref,
                   xbuf_ref, wg_st, wu_st, wd_st, wg_bf, wu_bf, wd_bf, gsem_ref, wsem_ref, *, rows, layer):
    b = pl.program_id(0)
    n_used = n_used_ref[0]
    slot = lax.rem(b, 2)

    def row_copy(blk, slt, r):
        return pltpu.make_async_copy(src_ref.at[pl.ds(tok_ref[blk * rows + r], 1), :],
                                     xbuf_ref.at[slt, pl.ds(r, 1), :], gsem_ref.at[slt])

    def for_valid_rows(blk, fn):
        def body(g, carry):
            for u in range(GATHER_UNROLL):
                fn(g * GATHER_UNROLL + u)
            return carry
        lax.fori_loop(0, (nvalid_ref[blk] + GATHER_UNROLL - 1) // GATHER_UNROLL, body, 0)

    def start_block(blk, slt):
        for_valid_rows(blk, lambda r: row_copy(blk, slt, r).start())

    def wait_block(blk, slt):
        for_valid_rows(blk, lambda r: row_copy(blk, slt, r).wait())

    def weight_copies(e):
        return (pltpu.make_async_copy(wg_hbm.at[layer, e], wg_st, wsem_ref.at[0]),
                pltpu.make_async_copy(wu_hbm.at[layer, e], wu_st, wsem_ref.at[1]),
                pltpu.make_async_copy(wd_hbm.at[layer, e], wd_st, wsem_ref.at[2]))

    def start_weights(e):
        for cp in weight_copies(e):
            cp.start(priority=1)

    @pl.when((b == 0) & (n_used > 0))
    def _():
        xbuf_ref[...] = jnp.zeros_like(xbuf_ref)
        start_block(0, 0)
        start_weights(blk_e_ref[0])

    @pl.when(b + 1 < n_used)
    def _():
        start_block(b + 1, 1 - slot)

    @pl.when((b < n_used) & (first_ref[b] == 1))
    def _():
        for cp in weight_copies(blk_e_ref[b]):
            cp.wait()
        wg_bf[...] = wg_st[...].astype(BF16)
        wu_bf[...] = wu_st[...].astype(BF16)
        wd_bf[...] = wd_st[...].astype(BF16)

        @pl.when(next_e_ref[b] >= 0)
        def _():
            start_weights(next_e_ref[b])

    @pl.when(b < n_used)
    def _():
        wait_block(b, slot)
        xa, xb = _unpack_bf16_pairs(xbuf_ref[slot])
        half = xa.shape[1]
        xa, xb = xa.astype(BF16), xb.astype(BF16)

        def proj(w_ref):
            return (jnp.dot(xa, w_ref[0:half, :], preferred_element_type=F32)
                    + jnp.dot(xb, w_ref[half:2 * half, :], preferred_element_type=F32))

        h = (_silu(proj(wg_bf)) * proj(wu_bf)).astype(BF16)
        y = jnp.dot(h, wd_bf[...], preferred_element_type=F32)
        o_ref[...] = _pack_bf16_pairs(y)

    @pl.when(b >= n_used)
    def _():
        o_ref[...] = jnp.zeros_like(o_ref)


def _expert_mlp(tables, packed, wg_all, wu_all, wd_all, layer):
    buf_tok, blk_e, nvalid, first, next_e, n_used = tables
    d, hid = wg_all.shape[2], wg_all.shape[3]
    rows = MOE_ROWS
    nr = buf_tok.shape[0]
    hbm = pl.BlockSpec(memory_space=pl.ANY)
    need = 3 * d * hid * (4 + 2) + 2 * rows * (d // 2) * 4 + 2 * rows * (d // 2) * 4 + rows * d * 8 + 6 * rows * hid * 4
    return pl.pallas_call(
        functools.partial(_expert_kernel, rows=rows, layer=layer),
        out_shape=jax.ShapeDtypeStruct((nr, d // 2), jnp.uint32),
        grid_spec=pltpu.PrefetchScalarGridSpec(
            num_scalar_prefetch=6, grid=(nr // rows,),
            in_specs=[hbm, hbm, hbm, hbm],
            out_specs=pl.BlockSpec((rows, d // 2), lambda b, *_: (b, 0)),
            scratch_shapes=[pltpu.VMEM((2, rows, d // 2), jnp.uint32),
                            pltpu.VMEM((d, hid), F32), pltpu.VMEM((d, hid), F32), pltpu.VMEM((hid, d), F32),
                            pltpu.VMEM((d, hid), BF16), pltpu.VMEM((d, hid), BF16), pltpu.VMEM((hid, d), BF16),
                            pltpu.SemaphoreType.DMA((2,)), pltpu.SemaphoreType.DMA((3,))]),
        compiler_params=pltpu.CompilerParams(dimension_semantics=("arbitrary",),
                                             vmem_limit_bytes=_vmem_limit(need)),
        name="moe_expert_mlp",
    )(buf_tok, blk_e, nvalid, first, next_e, n_used, packed, wg_all, wu_all, wd_all)


def _combine_kernel(d0_ref, d1_ref, yb_ref, x_ref, wt_ref, g_ref, nw_ref, o_ref, buf_ref, sem_ref,
                    *, rows, n_x, final_norm):
    i = pl.program_id(0)
    base = i * rows

    def copies(r):
        return (pltpu.make_async_copy(yb_ref.at[pl.ds(d0_ref[base + r], 1), :],
                                      buf_ref.at[0, pl.ds(r, 1), :], sem_ref.at[0]),
                pltpu.make_async_copy(yb_ref.at[pl.ds(d1_ref[base + r], 1), :],
                                      buf_ref.at[1, pl.ds(r, 1), :], sem_ref.at[1]))

    def start(r, carry):
        for k, cp in enumerate(copies(r)):
            cp.start(priority=k)
        return carry

    def wait(r, carry):
        for cp in copies(r):
            cp.wait()
        return carry

    lax.fori_loop(0, rows, start, 0, unroll=8)
    lax.fori_loop(0, rows, wait, 0, unroll=8)
    wt = wt_ref[...]
    w0, w1 = wt[:, 0:1], wt[:, 1:2]
    gate = _row_select(i, rows, n_x, g_ref)
    half = buf_ref.shape[2]
    halves = []
    for part0, part1, sl in zip(_unpack_bf16_pairs(buf_ref[0]), _unpack_bf16_pairs(buf_ref[1]),
                                (slice(0, half), slice(half, 2 * half))):
        halves.append(x_ref[:, sl] + gate[:, sl] * (part0 * w0 + part1 * w1))
    if final_norm:
        ssq = sum(jnp.sum(hv * hv, axis=-1, keepdims=True) for hv in halves)
        scale = lax.rsqrt(ssq / (2 * half) + EPS)
        halves = [hv * scale * nw_ref[:, sl] for hv, sl in zip(halves, (slice(0, half), slice(half, 2 * half)))]
    o_ref[:, 0:half] = halves[0]
    o_ref[:, half:2 * half] = halves[1]


def _combine(dest0, dest1, yb, x_stream, wts, gate2, norm_w, n_rows, n_x, final_norm):
    d = 2 * yb.shape[1]
    rows = MOE_ROWS
    return pl.pallas_call(
        functools.partial(_combine_kernel, rows=rows, n_x=n_x, final_norm=final_norm),
        out_shape=jax.ShapeDtypeStruct((n_rows, d), F32),
        grid_spec=pltpu.PrefetchScalarGridSpec(
            num_scalar_prefetch=2, grid=(n_rows // rows,),
            in_specs=[pl.BlockSpec(memory_space=pl.ANY),
                      pl.BlockSpec((rows, d), lambda i, a, b: (i, 0)),
                      pl.BlockSpec((rows, LANES), lambda i, a, b: (i, 0)),
                      pl.BlockSpec((2, d), lambda i, a, b: (0, 0)),
                      pl.BlockSpec((1, d), lambda i, a, b: (0, 0))],
            out_specs=pl.BlockSpec((rows, d), lambda i, a, b: (i, 0)),
            scratch_shapes=[pltpu.VMEM((2, rows, d // 2), jnp.uint32), pltpu.SemaphoreType.DMA((2,))]),
        compiler_params=pltpu.CompilerParams(dimension_semantics=("arbitrary",),
                                             vmem_limit_bytes=_vmem_limit(6 * rows * d * 4 + 2 * rows * d * 4)),
        name="moe_combine",
    )(dest0, dest1, yb, x_stream, wts, gate2, norm_w)


def _dispatch_tables(eid, n_tok):
    n_assign = n_tok * TOP_K
    flat_e = eid.reshape(-1)
    order = jnp.argsort(flat_e, stable=True)
    se = flat_e[order]
    counts = jnp.bincount(flat_e, length=N_EXPERTS)
    padded = (counts + MOE_ROWS - 1) // MOE_ROWS * MOE_ROWS
    pad_end = jnp.cumsum(padded)
    pad_start = pad_end - padded
    start = jnp.cumsum(counts) - counts
    dest_sorted = (pad_start[se] + jnp.arange(n_assign, dtype=I32) - start[se]).astype(I32)
    n_blocks = -(-n_assign // MOE_ROWS) + N_EXPERTS
    buf_tok = jnp.zeros((n_blocks * MOE_ROWS,), I32).at[dest_sorted].set((order // TOP_K).astype(I32))
    dest = jnp.zeros((n_assign,), I32).at[order].set(dest_sorted).reshape(n_tok, TOP_K)
    blk_e = jnp.minimum(jnp.searchsorted(pad_end, jnp.arange(n_blocks, dtype=I32) * MOE_ROWS, side='right'),
                        N_EXPERTS - 1).astype(I32)
    n_used = (pad_end[-1] // MOE_ROWS).astype(I32)
    blk_start = jnp.arange(n_blocks, dtype=I32) * MOE_ROWS
    nvalid = jnp.clip(counts[blk_e] - (blk_start - pad_start[blk_e]), 0, MOE_ROWS).astype(I32)
    first = jnp.concatenate([jnp.ones((1,), I32), (blk_e[1:] != blk_e[:-1]).astype(I32)])
    nxt = jnp.searchsorted(blk_e, blk_e, side='right').astype(I32)
    next_e = jnp.where(nxt < n_used, blk_e[jnp.minimum(nxt, n_blocks - 1)], -1).astype(I32)
    return (buf_tok, blk_e, nvalid, first, next_e, n_used.reshape(1)), dest


def _hier_moe(stream, n_rows, n_x, norm_w, shift2, scale2, gate2, w_r, b_r, wg_all, wu_all, wd_all, layer, final_w):
    packed, ids, wts = _norm_route(stream, norm_w, shift2, scale2, w_r, b_r, n_rows, n_x)
    tables, dest = _dispatch_tables(ids[:, :TOP_K], n_rows)
    yb = _expert_mlp(tables, packed, wg_all, wu_all, wd_all, layer)
    final_norm = final_w is not None
    nw = final_w if final_norm else norm_w
    return _combine(dest[:, 0], dest[:, 1], yb, stream, wts, gate2, nw, n_rows, n_x, final_norm)


def kernel(x, c, ctx, c_ctx, w_mod, b_mod, norm1_w, w_in, hgrn_lower_bounds, hgrn_norm_w, ssd_conv_w, ssd_conv_b,
           ssd_dt_bias, ssd_a_log, ssd_d, ssd_norm_w, w_out, norm2_w, w_group_router, b_group_router,
           w_expert_router, b_expert_router, w_gate, w_up, w_down, final_norm_w):
    bsz, seq, d = x.shape
    assert bsz == 1
    n_ctx = ctx.shape[1]
    depth = w_mod.shape[0]
    t_all = seq + n_ctx
    hg_w = d // 2
    ssd_w = d - hg_w
    n_ssd_heads = ssd_w // SSD_HEAD_DIM
    hg_cols = 5 * hg_w
    conv_dim = ssd_w + 2 * SSD_GROUPS * SSD_STATE
    assert seq // GRID_W == SSD_CHUNK and n_ctx % SSD_CHUNK == 0 and seq % n_ctx == 0
    gw = n_ssd_heads // SSD_GROUPS * SSD_HEAD_DIM
    dt_pad = LANES - 2 * n_ssd_heads

    stream = jnp.concatenate([x[0], ctx[0]], axis=0)
    c2t = jnp.stack([c[0], c_ctx], axis=1)
    lb_soft = jax.nn.softmax(hgrn_lower_bounds.astype(F32), axis=0)
    lower_bounds = jnp.cumsum(lb_soft, axis=0) - lb_soft[0]
    s_zero = jnp.zeros((SSD_GROUPS, SSD_STATE, gw), F32)
    w_in_t = jnp.swapaxes(w_in, 1, 2)

    for l in range(depth):
        last = l == depth - 1
        mod = _mod_vectors(c2t, w_mod, l, b_mod[l][None, :]).reshape(2, MOD_CHUNKS, d)
        chunk = lambda k: mod[:, k, :]

        h = _norm_mod(stream, norm1_w[l][None, :], chunk(0), chunk(1), t_all, seq, BF16)
        p_hg = _matmul_nt(h, w_in_t, l, 0, hg_cols)
        p_z = _matmul_nt(h, w_in_t, l, hg_cols, ssd_w)
        p_xbc = _matmul_nt(h, w_in_t, l, hg_cols + ssd_w, conv_dim)
        w_dt = jnp.pad(w_in_t[l, hg_cols + ssd_w + conv_dim:, :], ((0, dt_pad), (0, 0)))
        p_dt = _matmul_nt(h, w_dt[None], 0, 0, LANES)

        lb = lower_bounds[l]
        outs = []
        for direction in range(2):
            lbd = lb[direction][None, :]
            outs.append(_hgrn_scan(p_hg, 1 + direction, jnp.log(lbd), jnp.log1p(-lbd), 1.0 - lbd, seq,
                                   rev=direction == 1))
        o_fw, o_bw = outs

        xbc_x, xbc_c = _ssd_conv(p_xbc, ssd_conv_w[l], ssd_conv_b[l][None, :], seq)
        dt_bias = jnp.pad(ssd_dt_bias[l].reshape(1, -1), ((0, 0), (0, dt_pad)))
        a_row = jnp.pad(-jnp.exp(ssd_a_log[l].astype(F32)).reshape(1, -1), ((0, 0), (0, dt_pad)))
        dt_x = _permute_latent(p_dt, seq)
        dt_c = p_dt[seq:]
        ys_x, ys_c = [], []
        for direction in range(2):
            y_c, s_c = _ssd_scan(xbc_c, dt_c, dt_bias, a_row, s_zero, n_ssd_heads, direction, "context")
            y_x, _ = _ssd_scan(xbc_x, dt_x, dt_bias, a_row, s_c, n_ssd_heads, direction, "latent")
            ys_c.append(y_c)
            ys_x.append(y_x)
        d_row = jnp.repeat(ssd_d[l].astype(F32), SSD_HEAD_DIM)[None, :]
        y_x = _unpermute_latent(ys_x[0], ys_x[1], xbc_x, d_row)
        ab = _readout(o_fw, o_bw, p_hg, jnp.tile(hgrn_norm_w[l], hg_w // HG_HEAD_DIM)[None, :],
                      y_x, ys_c[0], ys_c[1], xbc_c, p_z, d_row, ssd_norm_w[l][None, :], seq)
        stream = _matmul_residual(ab, w_out, l, stream, chunk(2), seq)

        w_r = jnp.pad(jnp.concatenate([w_group_router[l], w_expert_router[l]], axis=1),
                      ((0, 0), (0, LANES - MOE_GROUPS - N_EXPERTS)))
        b_r = jnp.pad(jnp.concatenate([b_group_router[l], b_expert_router[l]]),
                      (0, LANES - MOE_GROUPS - N_EXPERTS))[None, :]
        n_rows = seq if last else t_all
        stream = _hier_moe(stream, n_rows, seq, norm2_w[l][None, :], chunk(3), chunk(4), chunk(5), w_r, b_r,
                           w_gate, w_up, w_down, l, final_norm_w[None, :] if last else None)
    return stream[:seq].reshape(bsz, seq, d)
```

```python
import functools

import numpy as np
import jax
import jax.numpy as jnp
from jax import lax
from jax.experimental import pallas as pl
from jax.experimental.pallas import tpu as pltpu

F32 = jnp.float32
BF16 = jnp.bfloat16
I32 = jnp.int32

EPS = 1e-6
GRID_W = 64
MOD_CHUNKS = 6

HG_HEAD_DIM = 128
SSD_HEAD_DIM = 64
SSD_GROUPS = 8
SSD_STATE = 128
MOE_GROUPS = 4
EXPERTS_PER_GROUP = 8
N_EXPERTS = MOE_GROUPS * EXPERTS_PER_GROUP
TOP_K = 2

LANES = 128
SUBLANES = 8
VMEM_BYTES_V7X = 64 * 1024 * 1024
HG_CHUNK = 64
HG_SUB = 16
HG_HEADS_PER_STEP = 8
HG_CHUNKS_PER_STEP = 2
SSD_CHUNK = 128
SSD_GROUPS_PER_STEP = 8
MOE_ROWS = 256


def _vmem_limit(nbytes):
    return int(min(max(nbytes * 3 // 2, 16 * 1024 * 1024), VMEM_BYTES_V7X - 8 * 1024 * 1024))


def _split3(x):
    h = x.astype(BF16)
    r = x - h.astype(F32)
    m = r.astype(BF16)
    lo = (r - m.astype(F32)).astype(BF16)
    return h, m, lo


def _dot3(a01, x, dims=None):
    out = None
    for part in _split3(x):
        if dims is None:
            t = jnp.dot(a01, part, preferred_element_type=F32)
        else:
            t = lax.dot_general(a01, part, dims, preferred_element_type=F32)
        out = t if out is None else out + t
    return out


def _dot3_rhs01(x, b01):
    out = None
    for part in _split3(x):
        t = jnp.dot(part, b01, preferred_element_type=F32)
        out = t if out is None else out + t
    return out


def _silu(x):
    return x * jax.nn.sigmoid(x)


def _pack_bf16_pairs(v):
    half = v.shape[1] // 2
    rb = v.astype(BF16).astype(F32)
    hi = lax.bitcast_convert_type(rb[:, :half], jnp.uint32)
    lo = lax.shift_right_logical(lax.bitcast_convert_type(rb[:, half:], jnp.uint32), jnp.uint32(16))
    return hi | lo


def _unpack_bf16_pairs(pk):
    first = lax.bitcast_convert_type(pk & jnp.uint32(0xFFFF0000), F32)
    second = lax.bitcast_convert_type(lax.shift_left(pk, jnp.uint32(16)), F32)
    return first, second


def _mod_kernel(c_ref, w_ref, b_ref, o_ref, acc_ref):
    k = pl.program_id(1)

    @pl.when(k == 0)
    def _():
        acc_ref[...] = jnp.zeros_like(acc_ref)

    c = c_ref[...]
    s = _silu(c)
    w = w_ref[...]
    tk, tn = w.shape
    for r in range(2):
        prod = w * s[:, r:r + 1]
        acc_ref[r] += jnp.sum(prod.reshape(tk // SUBLANES, SUBLANES, tn), axis=0)

    @pl.when(k == pl.num_programs(1) - 1)
    def _():
        o_ref[...] = jnp.sum(acc_ref[...], axis=1) + b_ref[...]


def _mod_vectors(c2t, w_all, layer, b):
    _, d, n = w_all.shape
    tk, tn = min(1024, d), min(2048, n)
    return pl.pallas_call(
        _mod_kernel,
        out_shape=jax.ShapeDtypeStruct((2, n), F32),
        grid=(n // tn, d // tk),
        in_specs=[pl.BlockSpec((tk, 2), lambda j, k: (k, 0)),
                  pl.BlockSpec((None, tk, tn), lambda j, k: (layer, k, j)),
                  pl.BlockSpec((1, tn), lambda j, k: (0, j))],
        out_specs=pl.BlockSpec((2, tn), lambda j, k: (0, j)),
        scratch_shapes=[pltpu.VMEM((2, SUBLANES, tn), F32)],
        compiler_params=pltpu.CompilerParams(
            dimension_semantics=("arbitrary", "arbitrary"),
            vmem_limit_bytes=_vmem_limit(2 * tk * tn * 4 + 4 * tk * tn)),
        name="adaln_matvec",
    )(c2t, w_all, b)


def _row_select(i, tm, n_x, ref):
    row = i * tm + lax.broadcasted_iota(I32, (tm, 1), 0)
    return jnp.where(row < n_x, ref[0:1, :], ref[1:2, :])


def _norm_mod_kernel(x_ref, w_ref, sh_ref, sc_ref, o_ref, *, n_x, tm):
    i = pl.program_id(0)
    x = x_ref[...]
    ms = jnp.mean(x * x, axis=-1, keepdims=True)
    y = x * lax.rsqrt(ms + EPS) * w_ref[...]
    sc = _row_select(i, tm, n_x, sc_ref)
    sh = _row_select(i, tm, n_x, sh_ref)
    o_ref[...] = (y * (1.0 + sc) + sh).astype(o_ref.dtype)


def _norm_mod(x, w, shift2, scale2, n_rows, n_x, out_dtype):
    d = x.shape[1]
    tm = 256
    return pl.pallas_call(
        functools.partial(_norm_mod_kernel, n_x=n_x, tm=tm),
        out_shape=jax.ShapeDtypeStruct((n_rows, d), out_dtype),
        grid=(n_rows // tm,),
        in_specs=[pl.BlockSpec((tm, d), lambda i: (i, 0)),
                  pl.BlockSpec((1, d), lambda i: (0, 0)),
                  pl.BlockSpec((2, d), lambda i: (0, 0)),
                  pl.BlockSpec((2, d), lambda i: (0, 0))],
        out_specs=pl.BlockSpec((tm, d), lambda i: (i, 0)),
        compiler_params=pltpu.CompilerParams(
            dimension_semantics=("arbitrary",),
            vmem_limit_bytes=_vmem_limit(4 * tm * d * 4 + 4 * tm * d * 4)),
        name="norm_modulate",
    )(x, w, shift2, scale2)


def _mm_nt_kernel(a_ref, wt_ref, o_ref, wb_ref):
    @pl.when(pl.program_id(1) == 0)
    def _():
        wb_ref[...] = wt_ref[...].astype(BF16)

    o_ref[...] = lax.dot_general(a_ref[...], wb_ref[...], (((1,), (1,)), ((), ())), preferred_element_type=F32)


def _mm_res_kernel(a_ref, w_ref, r_ref, g_ref, o_ref, wb_ref, *, n_x, tm):
    i = pl.program_id(1)

    @pl.when(i == 0)
    def _():
        wb_ref[...] = w_ref[...].astype(BF16)

    acc = jnp.dot(a_ref[...], wb_ref[...], preferred_element_type=F32)
    o_ref[...] = r_ref[...] + _row_select(i, tm, n_x, g_ref) * acc


def _mm_tiles(m, k, n):
    tm = 1056 if m % 1056 == 0 else (1024 if m % 1024 == 0 else m)
    tn = 512 if n % 512 == 0 else n
    need = 2 * k * tn * 4 + k * tn * 2 + 2 * tm * k * 2 + 4 * tm * tn * 4
    return tm, tn, need


def _matmul_nt(a, wt_all, layer, col_off, n):
    m, k = a.shape
    tm, tn, need = _mm_tiles(m, k, n)
    off = col_off // tn
    assert off * tn == col_off
    return pl.pallas_call(
        _mm_nt_kernel,
        out_shape=jax.ShapeDtypeStruct((m, n), F32),
        grid=(n // tn, m // tm),
        in_specs=[pl.BlockSpec((tm, k), lambda j, i: (i, 0)),
                  pl.BlockSpec((None, tn, k), lambda j, i: (layer, j + off, 0))],
        out_specs=pl.BlockSpec((tm, tn), lambda j, i: (i, j)),
        scratch_shapes=[pltpu.VMEM((tn, k), BF16)],
        compiler_params=pltpu.CompilerParams(
            dimension_semantics=("arbitrary", "arbitrary"), vmem_limit_bytes=_vmem_limit(need)),
        name="proj_matmul",
    )(a, wt_all)


def _matmul_residual(a, w_all, layer, res, gate2, n_x):
    m, k = a.shape
    n = w_all.shape[2]
    tm, tn, need = _mm_tiles(m, k, n)
    return pl.pallas_call(
        functools.partial(_mm_res_kernel, n_x=n_x, tm=tm),
        out_shape=jax.ShapeDtypeStruct((m, n), F32),
        grid=(n // tn, m // tm),
        in_specs=[pl.BlockSpec((tm, k), lambda j, i: (i, 0)),
                  pl.BlockSpec((None, k, tn), lambda j, i: (layer, 0, j)),
                  pl.BlockSpec((tm, tn), lambda j, i: (i, j)),
                  pl.BlockSpec((2, tn), lambda j, i: (0, j))],
        out_specs=pl.BlockSpec((tm, tn), lambda j, i: (i, j)),
        scratch_shapes=[pltpu.VMEM((k, tn), BF16)],
        compiler_params=pltpu.CompilerParams(
            dimension_semantics=("arbitrary", "arbitrary"), vmem_limit_bytes=_vmem_limit(need)),
        name="out_proj_residual",
    )(a, w_all, res, gate2)


def _hgrn_consts(rev):
    c, sub = HG_CHUNK, HG_SUB
    t = np.arange(c)
    if not rev:
        tri = t[None, :] <= t[:, None]
        mid = (t // sub) * sub + sub // 2 - 1
        mmid = t[None, :] <= mid[:, None]
    else:
        tri = t[None, :] >= t[:, None]
        mid = (t // sub) * sub + sub // 2
        mmid = t[None, :] >= mid[:, None]
    mcat = np.concatenate([tri, mmid, np.ones((c, c), bool)], axis=0).astype(np.float32)
    return jnp.asarray(mcat, BF16), jnp.asarray(tri.astype(np.float32), F32)


def _hgrn_kernel(q_ref, f_ref, v_ref, loglb_ref, l1mlb_ref, omlb_ref, mcat_ref, tri_ref,
                 o_ref, st_ref, *, rev, heads):
    hg = pl.program_id(0)
    j = pl.program_id(1)
    c, sub, dk = HG_CHUNK, HG_SUB, HG_HEAD_DIM
    nsub = c // sub

    @pl.when(j == 0)
    def _():
        st_ref[...] = jnp.zeros_like(st_ref)

    s = f_ref[...]
    e = jnp.exp(-jnp.abs(s))
    lse = jnp.log1p(e)
    log_sig = jnp.minimum(s, 0.0) - lse
    a = loglb_ref[...]
    cc = l1mlb_ref[...] + log_sig
    logf = jnp.maximum(a, cc) + jnp.log1p(jnp.exp(-jnp.abs(a - cc)))
    sig_neg = jnp.where(s >= 0.0, e, 1.0) / (1.0 + e)
    kk_all = omlb_ref[...] * sig_neg
    q_all = q_ref[...]
    vb_all = v_ref[...].astype(BF16)
    causal = tri_ref[...] > 0.5
    zero_tile = jnp.zeros((sub, dk), BF16)
    nck = q_all.shape[0] // c

    for ck in (range(nck - 1, -1, -1) if rev else range(nck)):
        rs = slice(ck * c, (ck + 1) * c)
        q, kk, vb = q_all[rs], kk_all[rs], vb_all[rs]
        ball3 = _dot3(mcat_ref[...], logf[rs])
        b, bmid, btot = ball3[0:c], ball3[c:2 * c], ball3[2 * c:3 * c]
        ka = (kk * jnp.exp(bmid - b)).astype(BF16)
        qs = (q * jnp.exp(b)).astype(BF16)
        ks = (kk * jnp.exp(btot - b)).astype(BF16)
        etot = jnp.exp(btot[0:1, :])

        for h in range(heads):
            sl = slice(h * dk, (h + 1) * dk)
            bh, qh, bmh, kah = b[:, sl], q[:, sl], bmid[:, sl], ka[:, sl]
            lhs_parts, rhs_rows = [], []
            for jb in range(nsub):
                r0, r1 = (jb * sub, c) if not rev else (0, (jb + 1) * sub)
                ref_row = bmh[jb * sub:jb * sub + 1, :]
                part = (qh[r0:r1] * jnp.exp(bh[r0:r1] - ref_row)).astype(BF16)
                pieces = []
                if r0 > 0:
                    pieces.append(jnp.zeros((r0, dk), BF16))
                pieces.append(part)
                if r1 < c:
                    pieces.append(jnp.zeros((c - r1, dk), BF16))
                lhs_parts.append(jnp.concatenate(pieces, axis=0) if len(pieces) > 1 else part)
                row = [zero_tile] * nsub
                row[jb] = kah[jb * sub:(jb + 1) * sub]
                rhs_rows.append(jnp.concatenate(row, axis=1))
            lhs = jnp.concatenate(lhs_parts, axis=1)
            rhs = jnp.concatenate(rhs_rows, axis=0)
            scores = lax.dot_general(lhs, rhs, (((1,), (1,)), ((), ())), preferred_element_type=F32)
            amat = jnp.where(causal, scores, 0.0).astype(BF16)
            hidx = hg * heads + h
            st = st_ref[hidx]
            out = jnp.dot(amat, vb[:, sl], preferred_element_type=F32)
            out = out + lax.dot_general(qs[:, sl], st.astype(BF16), (((1,), (1,)), ((), ())),
                                        preferred_element_type=F32)
            o_ref[rs, sl] = out
            upd = lax.dot_general(vb[:, sl], ks[:, sl], (((0,), (0,)), ((), ())), preferred_element_type=F32)
            st_ref[hidx] = st * etot[:, sl] + upd


def _hgrn_scan(p_hg, f_seg, loglb, l1mlb, omlb, n_x, rev):
    t_all, w = p_hg.shape[0], p_hg.shape[1] // 5
    c = HG_CHUNK
    heads = HG_HEADS_PER_STEP
    bw = heads * HG_HEAD_DIM
    nseg = w // bw
    n_heads = w // HG_HEAD_DIM
    rows = HG_CHUNKS_PER_STEP * c
    ncx, nct = n_x // rows, t_all // rows
    ncc = nct - ncx
    mcat, tri = _hgrn_consts(rev)

    def chunk(j):
        if rev:
            return jnp.where(j < ncc, nct - 1 - j, ncx - 1 - (j - ncc))
        return jnp.where(j < ncc, ncx + j, j - ncc)

    def seg_spec(seg):
        return pl.BlockSpec((rows, bw), lambda hg, j: (chunk(j), seg * nseg + hg))

    vec_spec = pl.BlockSpec((1, bw), lambda hg, j: (0, hg))
    return pl.pallas_call(
        functools.partial(_hgrn_kernel, rev=rev, heads=heads),
        out_shape=jax.ShapeDtypeStruct((t_all, w), F32),
        grid=(nseg, nct),
        in_specs=[seg_spec(0), seg_spec(f_seg), seg_spec(3), vec_spec, vec_spec, vec_spec,
                  pl.BlockSpec((3 * c, c), lambda hg, j: (0, 0)),
                  pl.BlockSpec((c, c), lambda hg, j: (0, 0))],
        out_specs=pl.BlockSpec((rows, bw), lambda hg, j: (chunk(j), hg)),
        scratch_shapes=[pltpu.VMEM((n_heads, HG_HEAD_DIM, HG_HEAD_DIM), F32)],
        compiler_params=pltpu.CompilerParams(
            dimension_semantics=("arbitrary", "arbitrary"),
            vmem_limit_bytes=_vmem_limit(24 * rows * bw * 4 + n_heads * HG_HEAD_DIM * HG_HEAD_DIM * 4)),
        name="hgrn2_scan_bw" if rev else "hgrn2_scan_fw",
    )(p_hg, p_hg, p_hg, loglb, l1mlb, omlb, mcat, tri)


def _shift_rows(u, down):
    n = u.shape[0]
    row = lax.broadcasted_iota(I32, u.shape, 0)
    if down:
        return jnp.where(row == 0, 0.0, pltpu.roll(u, 1, 0))
    return jnp.where(row == n - 1, 0.0, pltpu.roll(u, n - 1, 0))


def _to_scan_order(src_ref, dst_ref, rows):
    for col in range(GRID_W):
        dst_ref[col * rows:(col + 1) * rows, :] = src_ref[pl.ds(col, rows, stride=GRID_W), :]


def _conv_x_kernel(u_ref, w_ref, b_ref, o_ref, y_ref, *, rows):
    gw = GRID_W
    w0, w1, w2, bias = w_ref[0:1, :], w_ref[1:2, :], w_ref[2:3, :], b_ref[...]

    def slab(r):
        start = r * gw if isinstance(r, int) else pl.multiple_of(r * gw, gw)
        return pl.ds(start, gw)

    def emit(r, up, dn):
        y = w0 * up + w1 * u_ref[slab(r), :] + w2 * dn + bias
        y_ref[slab(r), :] = _silu(y)

    emit(0, _shift_rows(u_ref[slab(rows - 1), :], True), u_ref[slab(1), :])
    emit(rows - 1, u_ref[slab(rows - 2), :], _shift_rows(u_ref[slab(0), :], False))

    def body(r, carry):
        emit(r, u_ref[slab(r - 1), :], u_ref[slab(r + 1), :])
        return carry

    lax.fori_loop(1, rows - 1, body, 0)
    _to_scan_order(y_ref, o_ref, rows)


def _conv_c_kernel(u_ref, w_ref, b_ref, o_ref):
    u = u_ref[...]
    y = w_ref[0:1, :] * _shift_rows(u, True) + w_ref[1:2, :] * u + w_ref[2:3, :] * _shift_rows(u, False)
    o_ref[...] = _silu(y + b_ref[...])


def _permute_kernel(u_ref, o_ref, *, rows):
    _to_scan_order(u_ref, o_ref, rows)


def _unpermute_kernel(yf_ref, yb_ref, xs_ref, d_ref, o_ref, *, rows):
    d = d_ref[...]
    for col in range(GRID_W):
        sl = slice(col * rows, (col + 1) * rows)
        o_ref[pl.ds(col, rows, stride=GRID_W), :] = yf_ref[sl, :] + yb_ref[sl, :] + d * xs_ref[sl, :]


def _ssd_conv(p_xbc, conv_w, conv_b, n_x):
    t_all, f = p_xbc.shape
    rows = n_x // GRID_W
    fb = LANES
    blk = n_x * fb * 4
    xbc_x = pl.pallas_call(
        functools.partial(_conv_x_kernel, rows=rows),
        out_shape=jax.ShapeDtypeStruct((n_x, f), F32),
        grid=(f // fb,),
        in_specs=[pl.BlockSpec((n_x, fb), lambda k: (0, k)),
                  pl.BlockSpec((3, fb), lambda k: (0, k)),
                  pl.BlockSpec((1, fb), lambda k: (0, k))],
        out_specs=pl.BlockSpec((n_x, fb), lambda k: (0, k)),
        scratch_shapes=[pltpu.VMEM((n_x, fb), F32)],
        compiler_params=pltpu.CompilerParams(
            dimension_semantics=("arbitrary",), vmem_limit_bytes=_vmem_limit(5 * blk)),
        name="ssd_conv_latent",
    )(p_xbc, conv_w, conv_b)
    n_c = t_all - n_x
    fc = 1024
    xbc_c = pl.pallas_call(
        _conv_c_kernel,
        out_shape=jax.ShapeDtypeStruct((n_c, f), F32),
        grid=(f // fc,),
        in_specs=[pl.BlockSpec((n_c, fc), lambda k: (n_x // n_c, k)),
                  pl.BlockSpec((3, fc), lambda k: (0, k)),
                  pl.BlockSpec((1, fc), lambda k: (0, k))],
        out_specs=pl.BlockSpec((n_c, fc), lambda k: (0, k)),
        compiler_params=pltpu.CompilerParams(
            dimension_semantics=("arbitrary",), vmem_limit_bytes=_vmem_limit(4 * n_c * fc * 4)),
        name="ssd_conv_context",
    )(p_xbc, conv_w, conv_b)
    return xbc_x, xbc_c


def _permute_latent(u, n_x):
    rows = n_x // GRID_W
    return pl.pallas_call(
        functools.partial(_permute_kernel, rows=rows),
        out_shape=jax.ShapeDtypeStruct((n_x, u.shape[1]), F32),
        grid=(1,),
        in_specs=[pl.BlockSpec((n_x, u.shape[1]), lambda k: (0, 0))],
        out_specs=pl.BlockSpec((n_x, u.shape[1]), lambda k: (0, 0)),
        compiler_params=pltpu.CompilerParams(
            dimension_semantics=("arbitrary",), vmem_limit_bytes=_vmem_limit(4 * n_x * u.shape[1] * 4)),
        name="ssd_dt_to_scan_order",
    )(u)


def _unpermute_latent(y_fw, y_bw, xbc_x, d_row):
    n_x, w = y_fw.shape
    rows = n_x // GRID_W
    fb = LANES
    return pl.pallas_call(
        functools.partial(_unpermute_kernel, rows=rows),
        out_shape=jax.ShapeDtypeStruct((n_x, w), F32),
        grid=(w // fb,),
        in_specs=[pl.BlockSpec((n_x, fb), lambda k: (0, k)),
                  pl.BlockSpec((n_x, fb), lambda k: (0, k)),
                  pl.BlockSpec((n_x, fb), lambda k: (0, k)),
                  pl.BlockSpec((1, fb), lambda k: (0, k))],
        out_specs=pl.BlockSpec((n_x, fb), lambda k: (0, k)),
        compiler_params=pltpu.CompilerParams(
            dimension_semantics=("arbitrary",), vmem_limit_bytes=_vmem_limit(8 * n_x * fb * 4)),
        name="ssd_to_natural_order",
    )(y_fw, y_bw, xbc_x, d_row)


def _ssd_consts(rev):
    t = np.arange(SSD_CHUNK)
    tri = (t[None, :] >= t[:, None]) if rev else (t[None, :] <= t[:, None])
    return jnp.asarray(tri.astype(np.float32), BF16), jnp.asarray(tri.astype(np.float32), F32)


def _ssd_expanders(direction, n_heads):
    hpg = n_heads // SSD_GROUPS
    length = SSD_CHUNK
    e_lane = np.zeros((LANES, SSD_GROUPS * hpg * SSD_HEAD_DIM), np.float32)
    e_wide = np.zeros((LANES, SSD_GROUPS * hpg * length), np.float32)
    e_row = np.zeros((SSD_GROUPS * SUBLANES, LANES), np.float32)
    for g in range(SSD_GROUPS):
        for hh in range(hpg):
            col = direction * n_heads + g * hpg + hh
            e_lane[col, (g * hpg + hh) * SSD_HEAD_DIM:(g * hpg + hh + 1) * SSD_HEAD_DIM] = 1.0
            e_wide[col, (g * hpg + hh) * length:(g * hpg + hh + 1) * length] = 1.0
            e_row[g * SUBLANES + hh, col] = 1.0
    return jnp.asarray(e_lane, BF16), jnp.asarray(e_wide, BF16), jnp.asarray(e_row, BF16)


def _ssd_kernel(xs_ref, b_ref, c_ref, dt_ref, dtb_ref, a_ref, elane_ref, ewide_ref, erow_ref,
                tri01_ref, trif_ref, s0_ref, y_ref, sfin_ref, sgt_ref, *, rev, hpg, ngs):
    @pl.when(pl.program_id(1) == 0)
    def _():
        sgt_ref[...] = s0_ref[...]

    length = xs_ref.shape[0]
    p = SSD_HEAD_DIM
    gw = hpg * p
    nst = SSD_STATE
    dt_all = jax.nn.softplus(dt_ref[...] + dtb_ref[...])
    da_all = dt_all * a_ref[...]
    cum = _dot3(tri01_ref[...], da_all)
    tot = cum[0:1, :] if rev else cum[length - 1:length, :]
    dte = jnp.exp(tot - cum)
    ec = jnp.exp(cum)
    etot = jnp.broadcast_to(jnp.exp(tot), (SUBLANES, LANES))
    stacked = jnp.concatenate([dt_all, dte, ec, etot], axis=0)
    spread = _dot3_rhs01(stacked, elane_ref[...])
    dtb, dteb, ecb = spread[0:length], spread[length:2 * length], spread[2 * length:3 * length]
    cdb = spread[3 * length:3 * length + 1]
    cum_col = _dot3_rhs01(cum, ewide_ref[...])
    cum_row = _dot3(erow_ref[...], cum, (((1,), (1,)), ((), ())))

    xdt = xs_ref[...] * dtb
    xdtb = xdt.astype(BF16)
    xw = (xdt * dteb).astype(BF16)
    bmb, cmb = b_ref[...].astype(BF16), c_ref[...].astype(BF16)
    causal = trif_ref[...] > 0.5
    for g in range(ngs):
        bg, cg = bmb[:, g * nst:(g + 1) * nst], cmb[:, g * nst:(g + 1) * nst]
        cb = lax.dot_general(cg, bg, (((1,), (1,)), ((), ())), preferred_element_type=F32)
        ys = []
        for hh in range(hpg):
            hidx = g * hpg + hh
            seg = cum_col[:, hidx * length:(hidx + 1) * length] - cum_row[g * SUBLANES + hh:g * SUBLANES + hh + 1, :]
            dec = jnp.exp(jnp.where(causal, seg, -jnp.inf))
            m = (cb * dec).astype(BF16)
            ys.append(jnp.dot(m, xdtb[:, hidx * p:(hidx + 1) * p], preferred_element_type=F32))
        gsl = slice(g * gw, (g + 1) * gw)
        sgt = sgt_ref[g]
        y_off = jnp.dot(cg, sgt.astype(BF16), preferred_element_type=F32) * ecb[:, gsl]
        y_ref[:, gsl] = jnp.concatenate(ys, axis=1) + y_off
        upd = lax.dot_general(bg, xw[:, gsl], (((0,), (0,)), ((), ())), preferred_element_type=F32)
        sgt_new = sgt * cdb[:, gsl] + upd
        sgt_ref[g] = sgt_new
        sfin_ref[g] = sgt_new


def _ssd_scan(xbc, dt, dt_bias, a_row, s0, n_heads, direction, stream_name):
    n = xbc.shape[0]
    w = n_heads * SSD_HEAD_DIM
    hpg = n_heads // SSD_GROUPS
    gw = hpg * SSD_HEAD_DIM
    length = SSD_CHUNK
    nch = n // length
    rev = direction == 1
    e_lane, e_wide, e_row = _ssd_expanders(direction, n_heads)
    tri01, trif = _ssd_consts(rev)
    ngs = SSD_GROUPS_PER_STEP
    nb0 = w // (ngs * SSD_STATE)
    nbg = SSD_GROUPS // ngs

    def ch(j):
        return (nch - 1 - j) if rev else j

    const = lambda shape: pl.BlockSpec(shape, lambda g, j: (0,) * len(shape))
    return pl.pallas_call(
        functools.partial(_ssd_kernel, rev=rev, hpg=hpg, ngs=ngs),
        out_shape=(jax.ShapeDtypeStruct((n, w), F32),
                   jax.ShapeDtypeStruct((SSD_GROUPS, SSD_STATE, gw), F32)),
        grid=(nbg, nch),
        in_specs=[pl.BlockSpec((length, ngs * gw), lambda g, j: (ch(j), g)),
                  pl.BlockSpec((length, ngs * SSD_STATE), lambda g, j: (ch(j), nb0 + g)),
                  pl.BlockSpec((length, ngs * SSD_STATE), lambda g, j: (ch(j), nb0 + nbg + g)),
                  pl.BlockSpec((length, LANES), lambda g, j: (ch(j), 0)),
                  const((1, LANES)), const((1, LANES)),
                  pl.BlockSpec((LANES, ngs * gw), lambda g, j: (0, g)),
                  pl.BlockSpec((LANES, ngs * hpg * length), lambda g, j: (0, g)),
                  pl.BlockSpec((ngs * SUBLANES, LANES), lambda g, j: (g, 0)),
                  const((length, length)), const((length, length)),
                  pl.BlockSpec((ngs, SSD_STATE, gw), lambda g, j: (g, 0, 0))],
        out_specs=(pl.BlockSpec((length, ngs * gw), lambda g, j: (ch(j), g)),
                   pl.BlockSpec((ngs, SSD_STATE, gw), lambda g, j: (g, 0, 0))),
        scratch_shapes=[pltpu.VMEM((ngs, SSD_STATE, gw), F32)],
        compiler_params=pltpu.CompilerParams(dimension_semantics=("arbitrary", "arbitrary"),
                                             vmem_limit_bytes=_vmem_limit(24 * 1024 * 1024)),
        name="ssd_scan_%s_%s" % (stream_name, "bw" if rev else "fw"),
    )(xbc, xbc, xbc, dt, dt_bias, a_row, e_lane, e_wide, e_row, tri01, trif, s0)


def _readout_kernel(of_ref, ob_ref, g_ref, hw_ref, yx_ref, ycf_ref, ycb_ref, xsc_ref, z_ref, d_ref, sw_ref,
                    o_ref, *, n_xblk, hg_w, ssd_group):
    i = pl.program_id(0)
    o = of_ref[...] + ob_ref[...]
    gate = _silu(g_ref[...])
    hw = hw_ref[...]
    for h in range(hg_w // HG_HEAD_DIM):
        sl = slice(h * HG_HEAD_DIM, (h + 1) * HG_HEAD_DIM)
        oh = o[:, sl]
        ms = jnp.mean(oh * oh, axis=-1, keepdims=True)
        o_ref[:, sl] = (oh * lax.rsqrt(ms + EPS) * hw[:, sl] * gate[:, sl]).astype(o_ref.dtype)

    y_c = ycf_ref[...] + ycb_ref[...] + d_ref[...] * xsc_ref[...]
    y = jnp.where(i < n_xblk, yx_ref[...], y_c) * _silu(z_ref[...])
    sw = sw_ref[...]
    for g in range(y.shape[1] // ssd_group):
        sl = slice(g * ssd_group, (g + 1) * ssd_group)
        yg = y[:, sl]
        ms = jnp.mean(yg * yg, axis=-1, keepdims=True)
        o_ref[:, hg_w + g * ssd_group:hg_w + (g + 1) * ssd_group] = (
            yg * lax.rsqrt(ms + EPS) * sw[:, sl]).astype(o_ref.dtype)


def _readout(o_fw, o_bw, p_hg, hg_norm_w, y_x, yc_fw, yc_bw, xbc_c, p_z, d_row, ssd_norm_w, n_x):
    t_all, hg_w = o_fw.shape
    sw_w = p_z.shape[1]
    n_c = t_all - n_x
    tm = n_c
    n_xblk = n_x // tm
    xi = lambda i: (jnp.minimum(i, n_xblk - 1), 0)
    ci = lambda i: (0, 0)
    need = 2 * tm * 4 * (3 * hg_w + 5 * sw_w) + 2 * tm * (hg_w + sw_w) * 2 + 4 * tm * (hg_w + sw_w) * 4
    return pl.pallas_call(
        functools.partial(_readout_kernel, n_xblk=n_xblk, hg_w=hg_w, ssd_group=sw_w // SSD_GROUPS),
        out_shape=jax.ShapeDtypeStruct((t_all, hg_w + sw_w), BF16),
        grid=(t_all // tm,),
        in_specs=[pl.BlockSpec((tm, hg_w), lambda i: (i, 0)),
                  pl.BlockSpec((tm, hg_w), lambda i: (i, 0)),
                  pl.BlockSpec((tm, hg_w), lambda i: (i, 4)),
                  pl.BlockSpec((1, hg_w), lambda i: (0, 0)),
                  pl.BlockSpec((tm, sw_w), xi),
                  pl.BlockSpec((tm, sw_w), ci), pl.BlockSpec((tm, sw_w), ci), pl.BlockSpec((tm, sw_w), ci),
                  pl.BlockSpec((tm, sw_w), lambda i: (i, 0)),
                  pl.BlockSpec((1, sw_w), lambda i: (0, 0)),
                  pl.BlockSpec((1, sw_w), lambda i: (0, 0))],
        out_specs=pl.BlockSpec((tm, hg_w + sw_w), lambda i: (i, 0)),
        compiler_params=pltpu.CompilerParams(dimension_semantics=("arbitrary",),
                                             vmem_limit_bytes=_vmem_limit(need)),
        name="mixer_readout",
    )(o_fw, o_bw, p_hg, hg_norm_w, y_x, yc_fw, yc_bw, xbc_c, p_z, d_row, ssd_norm_w)


def _norm_route_kernel(x_ref, nw_ref, sh_ref, sc_ref, w_ref, b_ref, pk_ref, id_ref, wt_ref, *, n_x, tm):
    i = pl.program_id(0)
    x = x_ref[...]
    ms = jnp.mean(x * x, axis=-1, keepdims=True)
    f = x * lax.rsqrt(ms + EPS) * nw_ref[...]
    f = f * (1.0 + _row_select(i, tm, n_x, sc_ref)) + _row_select(i, tm, n_x, sh_ref)
    pk_ref[...] = _pack_bf16_pairs(f)

    logits = jnp.dot(f, w_ref[...], preferred_element_type=F32,
                     precision=lax.Precision.HIGHEST) + b_ref[...]
    lane_i = lax.broadcasted_iota(I32, (tm, LANES), 1)
    lane = lane_i.astype(F32)
    neg = -jnp.inf
    big = float(LANES)
    is_g = lane_i < MOE_GROUPS
    gl = jnp.where(is_g, logits, neg)
    gmax = jnp.max(gl, axis=-1, keepdims=True)
    gsum = jnp.sum(jnp.where(is_g, jnp.exp(gl - gmax), 0.0), axis=-1, keepdims=True)
    p_top = 1.0 / gsum
    g_idx = jnp.min(jnp.where(gl == gmax, lane, big), axis=-1, keepdims=True)
    e_lo = MOE_GROUPS + g_idx * EXPERTS_PER_GROUP
    in_grp = (lane >= e_lo) & (lane < e_lo + EXPERTS_PER_GROUP)
    el = jnp.where(in_grp, logits, neg)
    m1 = jnp.max(el, axis=-1, keepdims=True)
    i1 = jnp.min(jnp.where(in_grp & (el == m1), lane, big), axis=-1, keepdims=True)
    el2 = jnp.where(lane == i1, neg, el)
    m2 = jnp.max(el2, axis=-1, keepdims=True)
    i2 = jnp.min(jnp.where(in_grp & (lane != i1) & (el2 == m2), lane, big), axis=-1, keepdims=True)
    e2 = jnp.exp(m2 - m1)
    w1 = p_top / (1.0 + e2)
    w2 = p_top * e2 / (1.0 + e2)
    ids = jnp.where(lane_i == 0, i1 - MOE_GROUPS, jnp.where(lane_i == 1, i2 - MOE_GROUPS, 0.0))
    id_ref[...] = ids.astype(I32)
    wt_ref[...] = jnp.where(lane_i == 0, w1, jnp.where(lane_i == 1, w2, 0.0))


def _norm_route(x, norm_w, shift2, scale2, w_r, b_r, n_rows, n_x):
    d = x.shape[1]
    tm = 256
    row = lambda w: pl.BlockSpec((tm, w), lambda i: (i, 0))
    fixed = lambda r, w: pl.BlockSpec((r, w), lambda i: (0, 0))
    return pl.pallas_call(
        functools.partial(_norm_route_kernel, n_x=n_x, tm=tm),
        out_shape=(jax.ShapeDtypeStruct((n_rows, d // 2), jnp.uint32),
                   jax.ShapeDtypeStruct((n_rows, LANES), I32), jax.ShapeDtypeStruct((n_rows, LANES), F32)),
        grid=(n_rows // tm,),
        in_specs=[row(d), fixed(1, d), fixed(2, d), fixed(2, d), fixed(d, LANES), fixed(1, LANES)],
        out_specs=(row(d // 2), row(LANES), row(LANES)),
        compiler_params=pltpu.CompilerParams(dimension_semantics=("arbitrary",),
                                             vmem_limit_bytes=_vmem_limit(8 * tm * d * 4 + 4 * d * LANES * 4)),
        name="moe_norm_route",
    )(x, norm_w, shift2, scale2, w_r, b_r)


GATHER_UNROLL = 8


def _expert_kernel(tok_ref, blk_e_ref, nvalid_ref, first_ref, next_e_ref, n_used_ref,
                   src_ref, wg_hbm, wu_hbm, wd_hbm, o_ref,
                   xbuf_ref, wg_st, wu_st, wd_st, wg_bf, wu_bf, wd_bf, gsem_ref, wsem_ref, *, rows, layer):
    b = pl.program_id(0)
    n_used = n_used_ref[0]
    slot = lax.rem(b, 2)

    def row_copy(blk, slt, r):
        return pltpu.make_async_copy(src_ref.at[pl.ds(tok_ref[blk * rows + r], 1), :],
                                     xbuf_ref.at[slt, pl.ds(r, 1), :], gsem_ref.at[slt])

    def for_valid_rows(blk, fn):
        def body(g, carry):
            for u in range(GATHER_UNROLL):
                fn(g * GATHER_UNROLL + u)
            return carry
        lax.fori_loop(0, (nvalid_ref[blk] + GATHER_UNROLL - 1) // GATHER_UNROLL, body, 0)

    def start_block(blk, slt):
        for_valid_rows(blk, lambda r: row_copy(blk, slt, r).start())

    def wait_block(blk, slt):
        for_valid_rows(blk, lambda r: row_copy(blk, slt, r).wait())

    def weight_copies(e):
        return (pltpu.make_async_copy(wg_hbm.at[layer, e], wg_st, wsem_ref.at[0]),
                pltpu.make_async_copy(wu_hbm.at[layer, e], wu_st, wsem_ref.at[1]),
                pltpu.make_async_copy(wd_hbm.at[layer, e], wd_st, wsem_ref.at[2]))

    def start_weights(e):
        for cp in weight_copies(e):
            cp.start(priority=1)

    @pl.when((b == 0) & (n_used > 0))
    def _():
        xbuf_ref[...] = jnp.zeros_like(xbuf_ref)
        start_block(0, 0)
        start_weights(blk_e_ref[0])

    @pl.when(b + 1 < n_used)
    def _():
        start_block(b + 1, 1 - slot)

    @pl.when((b < n_used) & (first_ref[b] == 1))
    def _():
        for cp in weight_copies(blk_e_ref[b]):
            cp.wait()
        wg_bf[...] = wg_st[...].astype(BF16)
        wu_bf[...] = wu_st[...].astype(BF16)
        wd_bf[...] = wd_st[...].astype(BF16)

        @pl.when(next_e_ref[b] >= 0)
        def _():
            start_weights(next_e_ref[b])

    @pl.when(b < n_used)
    def _():
        wait_block(b, slot)
        xa, xb = _unpack_bf16_pairs(xbuf_ref[slot])
        half = xa.shape[1]
        xa, xb = xa.astype(BF16), xb.astype(BF16)

        def proj(w_ref):
            return (jnp.dot(xa, w_ref[0:half, :], preferred_element_type=F32)
                    + jnp.dot(xb, w_ref[half:2 * half, :], preferred_element_type=F32))

        h = (_silu(proj(wg_bf)) * proj(wu_bf)).astype(BF16)
        y = jnp.dot(h, wd_bf[...], preferred_element_type=F32)
        o_ref[...] = _pack_bf16_pairs(y)

    @pl.when(b >= n_used)
    def _():
        o_ref[...] = jnp.zeros_like(o_ref)


def _expert_mlp(tables, packed, wg_all, wu_all, wd_all, layer):
    buf_tok, blk_e, nvalid, first, next_e, n_used = tables
    d, hid = wg_all.shape[2], wg_all.shape[3]
    rows = MOE_ROWS
    nr = buf_tok.shape[0]
    hbm = pl.BlockSpec(memory_space=pl.ANY)
    need = 3 * d * hid * (4 + 2) + 2 * rows * (d // 2) * 4 + 2 * rows * (d // 2) * 4 + rows * d * 8 + 6 * rows * hid * 4
    return pl.pallas_call(
        functools.partial(_expert_kernel, rows=rows, layer=layer),
        out_shape=jax.ShapeDtypeStruct((nr, d // 2), jnp.uint32),
        grid_spec=pltpu.PrefetchScalarGridSpec(
            num_scalar_prefetch=6, grid=(nr // rows,),
            in_specs=[hbm, hbm, hbm, hbm],
            out_specs=pl.BlockSpec((rows, d // 2), lambda b, *_: (b, 0)),
            scratch_shapes=[pltpu.VMEM((2, rows, d // 2), jnp.uint32),
                            pltpu.VMEM((d, hid), F32), pltpu.VMEM((d, hid), F32), pltpu.VMEM((hid, d), F32),
                            pltpu.VMEM((d, hid), BF16), pltpu.VMEM((d, hid), BF16), pltpu.VMEM((hid, d), BF16),
                            pltpu.SemaphoreType.DMA((2,)), pltpu.SemaphoreType.DMA((3,))]),
        compiler_params=pltpu.CompilerParams(dimension_semantics=("arbitrary",),
                                             vmem_limit_bytes=_vmem_limit(need)),
        name="moe_expert_mlp",
    )(buf_tok, blk_e, nvalid, first, next_e, n_used, packed, wg_all, wu_all, wd_all)


def _combine_kernel(d0_ref, d1_ref, yb_ref, x_ref, wt_ref, g_ref, nw_ref, o_ref, buf_ref, sem_ref,
                    *, rows, n_x, final_norm):
    i = pl.program_id(0)
    base = i * rows

    def copies(r):
        return (pltpu.make_async_copy(yb_ref.at[pl.ds(d0_ref[base + r], 1), :],
                                      buf_ref.at[0, pl.ds(r, 1), :], sem_ref.at[0]),
                pltpu.make_async_copy(yb_ref.at[pl.ds(d1_ref[base + r], 1), :],
                                      buf_ref.at[1, pl.ds(r, 1), :], sem_ref.at[1]))

    def start(r, carry):
        for k, cp in enumerate(copies(r)):
            cp.start(priority=k)
        return carry

    def wait(r, carry):
        for cp in copies(r):
            cp.wait()
        return carry

    lax.fori_loop(0, rows, start, 0, unroll=8)
    lax.fori_loop(0, rows, wait, 0, unroll=8)
    wt = wt_ref[...]
    w0, w1 = wt[:, 0:1], wt[:, 1:2]
    gate = _row_select(i, rows, n_x, g_ref)
    half = buf_ref.shape[2]
    halves = []
    for part0, part1, sl in zip(_unpack_bf16_pairs(buf_ref[0]), _unpack_bf16_pairs(buf_ref[1]),
                                (slice(0, half), slice(half, 2 * half))):
        halves.append(x_ref[:, sl] + gate[:, sl] * (part0 * w0 + part1 * w1))
    if final_norm:
        ssq = sum(jnp.sum(hv * hv, axis=-1, keepdims=True) for hv in halves)
        scale = lax.rsqrt(ssq / (2 * half) + EPS)
        halves = [hv * scale * nw_ref[:, sl] for hv, sl in zip(halves, (slice(0, half), slice(half, 2 * half)))]
    o_ref[:, 0:half] = halves[0]
    o_ref[:, half:2 * half] = halves[1]


def _combine(dest0, dest1, yb, x_stream, wts, gate2, norm_w, n_rows, n_x, final_norm):
    d = 2 * yb.shape[1]
    rows = MOE_ROWS
    return pl.pallas_call(
        functools.partial(_combine_kernel, rows=rows, n_x=n_x, final_norm=final_norm),
        out_shape=jax.ShapeDtypeStruct((n_rows, d), F32),
        grid_spec=pltpu.PrefetchScalarGridSpec(
            num_scalar_prefetch=2, grid=(n_rows // rows,),
            in_specs=[pl.BlockSpec(memory_space=pl.ANY),
                      pl.BlockSpec((rows, d), lambda i, a, b: (i, 0)),
                      pl.BlockSpec((rows, LANES), lambda i, a, b: (i, 0)),
                      pl.BlockSpec((2, d), lambda i, a, b: (0, 0)),
                      pl.BlockSpec((1, d), lambda i, a, b: (0, 0))],
            out_specs=pl.BlockSpec((rows, d), lambda i, a, b: (i, 0)),
            scratch_shapes=[pltpu.VMEM((2, rows, d // 2), jnp.uint32), pltpu.SemaphoreType.DMA((2,))]),
        compiler_params=pltpu.CompilerParams(dimension_semantics=("arbitrary",),
                                             vmem_limit_bytes=_vmem_limit(6 * rows * d * 4 + 2 * rows * d * 4)),
        name="moe_combine",
    )(dest0, dest1, yb, x_stream, wts, gate2, norm_w)


def _dispatch_tables(eid, n_tok):
    n_assign = n_tok * TOP_K
    flat_e = eid.reshape(-1)
    order = jnp.argsort(flat_e, stable=True)
    se = flat_e[order]
    counts = jnp.bincount(flat_e, length=N_EXPERTS)
    padded = (counts + MOE_ROWS - 1) // MOE_ROWS * MOE_ROWS
    pad_end = jnp.cumsum(padded)
    pad_start = pad_end - padded
    start = jnp.cumsum(counts) - counts
    dest_sorted = (pad_start[se] + jnp.arange(n_assign, dtype=I32) - start[se]).astype(I32)
    dest = dest_sorted[jnp.argsort(order)].reshape(n_tok, TOP_K)
    n_blocks = -(-n_assign // MOE_ROWS) + N_EXPERTS
    blk_start = jnp.arange(n_blocks, dtype=I32) * MOE_ROWS
    blk_e = jnp.minimum(jnp.searchsorted(pad_end, blk_start, side='right'), N_EXPERTS - 1).astype(I32)
    n_used = (pad_end[-1] // MOE_ROWS).astype(I32)
    nvalid = jnp.clip(counts[blk_e] - (blk_start - pad_start[blk_e]), 0, MOE_ROWS).astype(I32)
    row_e = jnp.repeat(blk_e, MOE_ROWS)
    rank = jnp.arange(n_blocks * MOE_ROWS, dtype=I32) - pad_start[row_e]
    src = jnp.clip(start[row_e] + rank, 0, n_assign - 1)
    buf_tok = jnp.where(rank < counts[row_e], order[src] // TOP_K, 0).astype(I32)
    first = jnp.concatenate([jnp.ones((1,), I32), (blk_e[1:] != blk_e[:-1]).astype(I32)])
    nxt = jnp.searchsorted(blk_e, blk_e, side='right').astype(I32)
    next_e = jnp.where(nxt < n_used, blk_e[jnp.minimum(nxt, n_blocks - 1)], -1).astype(I32)
    return (buf_tok, blk_e, nvalid, first, next_e, n_used.reshape(1)), dest


def _hier_moe(stream, n_rows, n_x, norm_w, shift2, scale2, gate2, w_r, b_r, wg_all, wu_all, wd_all, layer, final_w):
    packed, ids, wts = _norm_route(stream, norm_w, shift2, scale2, w_r, b_r, n_rows, n_x)
    tables, dest = _dispatch_tables(ids[:, :TOP_K], n_rows)
    yb = _expert_mlp(tables, packed, wg_all, wu_all, wd_all, layer)
    final_norm = final_w is not None
    nw = final_w if final_norm else norm_w
    return _combine(dest[:, 0], dest[:, 1], yb, stream, wts, gate2, nw, n_rows, n_x, final_norm)


def kernel(x, c, ctx, c_ctx, w_mod, b_mod, norm1_w, w_in, hgrn_lower_bounds, hgrn_norm_w, ssd_conv_w, ssd_conv_b,
           ssd_dt_bias, ssd_a_log, ssd_d, ssd_norm_w, w_out, norm2_w, w_group_router, b_group_router,
           w_expert_router, b_expert_router, w_gate, w_up, w_down, final_norm_w):
    bsz, seq, d = x.shape
    assert bsz == 1
    n_ctx = ctx.shape[1]
    depth = w_mod.shape[0]
    t_all = seq + n_ctx
    hg_w = d // 2
    ssd_w = d - hg_w
    n_ssd_heads = ssd_w // SSD_HEAD_DIM
    hg_cols = 5 * hg_w
    conv_dim = ssd_w + 2 * SSD_GROUPS * SSD_STATE
    assert seq // GRID_W == SSD_CHUNK and n_ctx % SSD_CHUNK == 0 and seq % n_ctx == 0
    gw = n_ssd_heads // SSD_GROUPS * SSD_HEAD_DIM
    dt_pad = LANES - 2 * n_ssd_heads

    stream = jnp.concatenate([x[0], ctx[0]], axis=0)
    c2t = jnp.stack([c[0], c_ctx], axis=1)
    lb_soft = jax.nn.softmax(hgrn_lower_bounds.astype(F32), axis=0)
    lower_bounds = jnp.cumsum(lb_soft, axis=0) - lb_soft[0]
    s_zero = jnp.zeros((SSD_GROUPS, SSD_STATE, gw), F32)
    w_in_t = jnp.swapaxes(w_in, 1, 2)

    for l in range(depth):
        last = l == depth - 1
        mod = _mod_vectors(c2t, w_mod, l, b_mod[l][None, :]).reshape(2, MOD_CHUNKS, d)
        chunk = lambda k: mod[:, k, :]

        h = _norm_mod(stream, norm1_w[l][None, :], chunk(0), chunk(1), t_all, seq, BF16)
        p_hg = _matmul_nt(h, w_in_t, l, 0, hg_cols)
        p_z = _matmul_nt(h, w_in_t, l, hg_cols, ssd_w)
        p_xbc = _matmul_nt(h, w_in_t, l, hg_cols + ssd_w, conv_dim)
        w_dt = jnp.pad(w_in_t[l, hg_cols + ssd_w + conv_dim:, :], ((0, dt_pad), (0, 0)))
        p_dt = _matmul_nt(h, w_dt[None], 0, 0, LANES)

        lb = lower_bounds[l]
        outs = []
        for direction in range(2):
            lbd = lb[direction][None, :]
            outs.append(_hgrn_scan(p_hg, 1 + direction, jnp.log(lbd), jnp.log1p(-lbd), 1.0 - lbd, seq,
                                   rev=direction == 1))
        o_fw, o_bw = outs

        xbc_x, xbc_c = _ssd_conv(p_xbc, ssd_conv_w[l], ssd_conv_b[l][None, :], seq)
        dt_bias = jnp.pad(ssd_dt_bias[l].reshape(1, -1), ((0, 0), (0, dt_pad)))
        a_row = jnp.pad(-jnp.exp(ssd_a_log[l].astype(F32)).reshape(1, -1), ((0, 0), (0, dt_pad)))
        dt_x = _permute_latent(p_dt, seq)
        dt_c = p_dt[seq:]
        ys_x, ys_c = [], []
        for direction in range(2):
            y_c, s_c = _ssd_scan(xbc_c, dt_c, dt_bias, a_row, s_zero, n_ssd_heads, direction, "context")
            y_x, _ = _ssd_scan(xbc_x, dt_x, dt_bias, a_row, s_c, n_ssd_heads, direction, "latent")
            ys_c.append(y_c)
            ys_x.append(y_x)
        d_row = jnp.repeat(ssd_d[l].astype(F32), SSD_HEAD_DIM)[None, :]
        y_x = _unpermute_latent(ys_x[0], ys_x[1], xbc_x, d_row)
        ab = _readout(o_fw, o_bw, p_hg, jnp.tile(hgrn_norm_w[l], hg_w // HG_HEAD_DIM)[None, :],
                      y_x, ys_c[0], ys_c[1], xbc_c, p_z, d_row, ssd_norm_w[l][None, :], seq)
        stream = _matmul_residual(ab, w_out, l, stream, chunk(2), seq)

        w_r = jnp.pad(jnp.concatenate([w_group_router[l], w_expert_router[l]], axis=1),
                      ((0, 0), (0, LANES - MOE_GROUPS - N_EXPERTS)))
        b_r = jnp.pad(jnp.concatenate([b_group_router[l], b_expert_router[l]]),
                      (0, LANES - MOE_GROUPS - N_EXPERTS))[None, :]
        n_rows = seq if last else t_all
        stream = _hier_moe(stream, n_rows, seq, norm2_w[l][None, :], chunk(3), chunk(4), chunk(5), w_r, b_r,
                           w_gate, w_up, w_down, l, final_norm_w[None, :] if last else None)
    return stream[:seq].reshape(bsz, seq, d)
```

```python
import functools

import numpy as np
import jax
import jax.numpy as jnp
from jax import lax
from jax.experimental import pallas as pl
from jax.experimental.pallas import tpu as pltpu

F32 = jnp.float32
BF16 = jnp.bfloat16
I32 = jnp.int32

EPS = 1e-6
GRID_W = 64
MOD_CHUNKS = 6

HG_HEAD_DIM = 128
SSD_HEAD_DIM = 64
SSD_GROUPS = 8
SSD_STATE = 128
MOE_GROUPS = 4
EXPERTS_PER_GROUP = 8
N_EXPERTS = MOE_GROUPS * EXPERTS_PER_GROUP
TOP_K = 2

LANES = 128
SUBLANES = 8
VMEM_BYTES_V7X = 64 * 1024 * 1024
HG_CHUNK = 64
HG_SUB = 16
HG_HEADS_PER_STEP = 8
HG_CHUNKS_PER_STEP = 2
SSD_CHUNK = 128
SSD_GROUPS_PER_STEP = 8
MOE_ROWS = 256
COMBINE_ROWS_PER_TRIP = 16


def _vmem_limit(nbytes):
    return int(min(max(nbytes * 3 // 2, 16 * 1024 * 1024), VMEM_BYTES_V7X - 8 * 1024 * 1024))


def _split3(x):
    h = x.astype(BF16)
    r = x - h.astype(F32)
    m = r.astype(BF16)
    lo = (r - m.astype(F32)).astype(BF16)
    return h, m, lo


def _dot3(a01, x, dims=None):
    out = None
    for part in _split3(x):
        if dims is None:
            t = jnp.dot(a01, part, preferred_element_type=F32)
        else:
            t = lax.dot_general(a01, part, dims, preferred_element_type=F32)
        out = t if out is None else out + t
    return out


def _dot3_rhs01(x, b01):
    out = None
    for part in _split3(x):
        t = jnp.dot(part, b01, preferred_element_type=F32)
        out = t if out is None else out + t
    return out


def _silu(x):
    return x * jax.nn.sigmoid(x)


def _pack_bf16_pairs(v):
    half = v.shape[1] // 2
    rb = v.astype(BF16).astype(F32)
    hi = lax.bitcast_convert_type(rb[:, :half], jnp.uint32)
    lo = lax.shift_right_logical(lax.bitcast_convert_type(rb[:, half:], jnp.uint32), jnp.uint32(16))
    return hi | lo


def _unpack_bf16_pairs(pk):
    first = lax.bitcast_convert_type(pk & jnp.uint32(0xFFFF0000), F32)
    second = lax.bitcast_convert_type(lax.shift_left(pk, jnp.uint32(16)), F32)
    return first, second


def _mod_kernel(c_ref, w_ref, b_ref, o_ref, acc_ref):
    k = pl.program_id(1)

    @pl.when(k == 0)
    def _():
        acc_ref[...] = jnp.zeros_like(acc_ref)

    c = c_ref[...]
    s = _silu(c)
    w = w_ref[...]
    tk, tn = w.shape
    for r in range(2):
        prod = w * s[:, r:r + 1]
        acc_ref[r] += jnp.sum(prod.reshape(tk // SUBLANES, SUBLANES, tn), axis=0)

    @pl.when(k == pl.num_programs(1) - 1)
    def _():
        o_ref[...] = jnp.sum(acc_ref[...], axis=1) + b_ref[...]


def _mod_vectors(c2t, w_all, layer, b):
    _, d, n = w_all.shape
    tk, tn = min(1024, d), min(2048, n)
    return pl.pallas_call(
        _mod_kernel,
        out_shape=jax.ShapeDtypeStruct((2, n), F32),
        grid=(n // tn, d // tk),
        in_specs=[pl.BlockSpec((tk, 2), lambda j, k: (k, 0)),
                  pl.BlockSpec((None, tk, tn), lambda j, k: (layer, k, j)),
                  pl.BlockSpec((1, tn), lambda j, k: (0, j))],
        out_specs=pl.BlockSpec((2, tn), lambda j, k: (0, j)),
        scratch_shapes=[pltpu.VMEM((2, SUBLANES, tn), F32)],
        compiler_params=pltpu.CompilerParams(
            dimension_semantics=("arbitrary", "arbitrary"),
            vmem_limit_bytes=_vmem_limit(2 * tk * tn * 4 + 4 * tk * tn)),
        name="adaln_matvec",
    )(c2t, w_all, b)


def _row_select(i, tm, n_x, ref):
    row = i * tm + lax.broadcasted_iota(I32, (tm, 1), 0)
    return jnp.where(row < n_x, ref[0:1, :], ref[1:2, :])


def _norm_mod_kernel(x_ref, w_ref, sh_ref, sc_ref, o_ref, *, n_x, tm):
    i = pl.program_id(0)
    x = x_ref[...]
    ms = jnp.mean(x * x, axis=-1, keepdims=True)
    y = x * lax.rsqrt(ms + EPS) * w_ref[...]
    sc = _row_select(i, tm, n_x, sc_ref)
    sh = _row_select(i, tm, n_x, sh_ref)
    o_ref[...] = (y * (1.0 + sc) + sh).astype(o_ref.dtype)


def _norm_mod(x, w, shift2, scale2, n_rows, n_x, out_dtype):
    d = x.shape[1]
    tm = 256
    return pl.pallas_call(
        functools.partial(_norm_mod_kernel, n_x=n_x, tm=tm),
        out_shape=jax.ShapeDtypeStruct((n_rows, d), out_dtype),
        grid=(n_rows // tm,),
        in_specs=[pl.BlockSpec((tm, d), lambda i: (i, 0)),
                  pl.BlockSpec((1, d), lambda i: (0, 0)),
                  pl.BlockSpec((2, d), lambda i: (0, 0)),
                  pl.BlockSpec((2, d), lambda i: (0, 0))],
        out_specs=pl.BlockSpec((tm, d), lambda i: (i, 0)),
        compiler_params=pltpu.CompilerParams(
            dimension_semantics=("arbitrary",),
            vmem_limit_bytes=_vmem_limit(4 * tm * d * 4 + 4 * tm * d * 4)),
        name="norm_modulate",
    )(x, w, shift2, scale2)


def _mm_nt_kernel(a_ref, wt_ref, o_ref, wb_ref):
    @pl.when(pl.program_id(1) == 0)
    def _():
        wb_ref[...] = wt_ref[...].astype(BF16)

    o_ref[...] = lax.dot_general(a_ref[...], wb_ref[...], (((1,), (1,)), ((), ())), preferred_element_type=F32)


def _mm_res_kernel(a_ref, w_ref, r_ref, g_ref, o_ref, wb_ref, *, n_x, tm):
    i = pl.program_id(1)

    @pl.when(i == 0)
    def _():
        wb_ref[...] = w_ref[...].astype(BF16)

    acc = jnp.dot(a_ref[...], wb_ref[...], preferred_element_type=F32)
    o_ref[...] = r_ref[...] + _row_select(i, tm, n_x, g_ref) * acc


def _mm_tiles(m, k, n):
    tm = 1056 if m % 1056 == 0 else (1024 if m % 1024 == 0 else m)
    tn = 512 if n % 512 == 0 else n
    need = 2 * k * tn * 4 + k * tn * 2 + 2 * tm * k * 2 + 4 * tm * tn * 4
    return tm, tn, need


def _matmul_nt(a, wt_all, layer, col_off, n):
    m, k = a.shape
    tm, tn, need = _mm_tiles(m, k, n)
    off = col_off // tn
    assert off * tn == col_off
    return pl.pallas_call(
        _mm_nt_kernel,
        out_shape=jax.ShapeDtypeStruct((m, n), F32),
        grid=(n // tn, m // tm),
        in_specs=[pl.BlockSpec((tm, k), lambda j, i: (i, 0)),
                  pl.BlockSpec((None, tn, k), lambda j, i: (layer, j + off, 0))],
        out_specs=pl.BlockSpec((tm, tn), lambda j, i: (i, j)),
        scratch_shapes=[pltpu.VMEM((tn, k), BF16)],
        compiler_params=pltpu.CompilerParams(
            dimension_semantics=("arbitrary", "arbitrary"), vmem_limit_bytes=_vmem_limit(need)),
        name="proj_matmul",
    )(a, wt_all)


def _matmul_residual(a, w_all, layer, res, gate2, n_x):
    m, k = a.shape
    n = w_all.shape[2]
    tm, tn, need = _mm_tiles(m, k, n)
    return pl.pallas_call(
        functools.partial(_mm_res_kernel, n_x=n_x, tm=tm),
        out_shape=jax.ShapeDtypeStruct((m, n), F32),
        grid=(n // tn, m // tm),
        in_specs=[pl.BlockSpec((tm, k), lambda j, i: (i, 0)),
                  pl.BlockSpec((None, k, tn), lambda j, i: (layer, 0, j)),
                  pl.BlockSpec((tm, tn), lambda j, i: (i, j)),
                  pl.BlockSpec((2, tn), lambda j, i: (0, j))],
        out_specs=pl.BlockSpec((tm, tn), lambda j, i: (i, j)),
        scratch_shapes=[pltpu.VMEM((k, tn), BF16)],
        compiler_params=pltpu.CompilerParams(
            dimension_semantics=("arbitrary", "arbitrary"), vmem_limit_bytes=_vmem_limit(need)),
        name="out_proj_residual",
    )(a, w_all, res, gate2)


def _hgrn_consts(rev):
    c, sub = HG_CHUNK, HG_SUB
    t = np.arange(c)
    if not rev:
        tri = t[None, :] <= t[:, None]
        mid = (t // sub) * sub + sub // 2 - 1
        mmid = t[None, :] <= mid[:, None]
    else:
        tri = t[None, :] >= t[:, None]
        mid = (t // sub) * sub + sub // 2
        mmid = t[None, :] >= mid[:, None]
    mcat = np.concatenate([tri, mmid, np.ones((c, c), bool)], axis=0).astype(np.float32)
    return jnp.asarray(mcat, BF16), jnp.asarray(tri.astype(np.float32), F32)


def _hgrn_kernel(q_ref, f_ref, v_ref, loglb_ref, l1mlb_ref, omlb_ref, mcat_ref, tri_ref,
                 o_ref, st_ref, *, rev, heads):
    hg = pl.program_id(0)
    j = pl.program_id(1)
    c, sub, dk = HG_CHUNK, HG_SUB, HG_HEAD_DIM
    nsub = c // sub

    @pl.when(j == 0)
    def _():
        st_ref[...] = jnp.zeros_like(st_ref)

    s = f_ref[...]
    e = jnp.exp(-jnp.abs(s))
    lse = jnp.log1p(e)
    log_sig = jnp.minimum(s, 0.0) - lse
    a = loglb_ref[...]
    cc = l1mlb_ref[...] + log_sig
    logf = jnp.maximum(a, cc) + jnp.log1p(jnp.exp(-jnp.abs(a - cc)))
    sig_neg = jnp.where(s >= 0.0, e, 1.0) / (1.0 + e)
    kk_all = omlb_ref[...] * sig_neg
    q_all = q_ref[...]
    vb_all = v_ref[...].astype(BF16)
    causal = tri_ref[...] > 0.5
    zero_tile = jnp.zeros((sub, dk), BF16)
    nck = q_all.shape[0] // c

    for ck in (range(nck - 1, -1, -1) if rev else range(nck)):
        rs = slice(ck * c, (ck + 1) * c)
        q, kk, vb = q_all[rs], kk_all[rs], vb_all[rs]
        ball3 = _dot3(mcat_ref[...], logf[rs])
        b, bmid, btot = ball3[0:c], ball3[c:2 * c], ball3[2 * c:3 * c]
        ka = (kk * jnp.exp(bmid - b)).astype(BF16)
        qs = (q * jnp.exp(b)).astype(BF16)
        ks = (kk * jnp.exp(btot - b)).astype(BF16)
        etot = jnp.exp(btot[0:1, :])

        for h in range(heads):
            sl = slice(h * dk, (h + 1) * dk)
            bh, qh, bmh, kah = b[:, sl], q[:, sl], bmid[:, sl], ka[:, sl]
            lhs_parts, rhs_rows = [], []
            for jb in range(nsub):
                r0, r1 = (jb * sub, c) if not rev else (0, (jb + 1) * sub)
                ref_row = bmh[jb * sub:jb * sub + 1, :]
                part = (qh[r0:r1] * jnp.exp(bh[r0:r1] - ref_row)).astype(BF16)
                pieces = []
                if r0 > 0:
                    pieces.append(jnp.zeros((r0, dk), BF16))
                pieces.append(part)
                if r1 < c:
                    pieces.append(jnp.zeros((c - r1, dk), BF16))
                lhs_parts.append(jnp.concatenate(pieces, axis=0) if len(pieces) > 1 else part)
                row = [zero_tile] * nsub
                row[jb] = kah[jb * sub:(jb + 1) * sub]
                rhs_rows.append(jnp.concatenate(row, axis=1))
            lhs = jnp.concatenate(lhs_parts, axis=1)
            rhs = jnp.concatenate(rhs_rows, axis=0)
            scores = lax.dot_general(lhs, rhs, (((1,), (1,)), ((), ())), preferred_element_type=F32)
            amat = jnp.where(causal, scores, 0.0).astype(BF16)
            hidx = hg * heads + h
            st = st_ref[hidx]
            out = jnp.dot(amat, vb[:, sl], preferred_element_type=F32)
            out = out + lax.dot_general(qs[:, sl], st.astype(BF16), (((1,), (1,)), ((), ())),
                                        preferred_element_type=F32)
            o_ref[rs, sl] = out
            upd = lax.dot_general(vb[:, sl], ks[:, sl], (((0,), (0,)), ((), ())), preferred_element_type=F32)
            st_ref[hidx] = st * etot[:, sl] + upd


def _hgrn_scan(p_hg, f_seg, loglb, l1mlb, omlb, n_x, rev):
    t_all, w = p_hg.shape[0], p_hg.shape[1] // 5
    c = HG_CHUNK
    heads = HG_HEADS_PER_STEP
    bw = heads * HG_HEAD_DIM
    nseg = w // bw
    n_heads = w // HG_HEAD_DIM
    rows = HG_CHUNKS_PER_STEP * c
    ncx, nct = n_x // rows, t_all // rows
    ncc = nct - ncx
    mcat, tri = _hgrn_consts(rev)

    def chunk(j):
        if rev:
            return jnp.where(j < ncc, nct - 1 - j, ncx - 1 - (j - ncc))
        return jnp.where(j < ncc, ncx + j, j - ncc)

    def seg_spec(seg):
        return pl.BlockSpec((rows, bw), lambda hg, j: (chunk(j), seg * nseg + hg))

    vec_spec = pl.BlockSpec((1, bw), lambda hg, j: (0, hg))
    return pl.pallas_call(
        functools.partial(_hgrn_kernel, rev=rev, heads=heads),
        out_shape=jax.ShapeDtypeStruct((t_all, w), F32),
        grid=(nseg, nct),
        in_specs=[seg_spec(0), seg_spec(f_seg), seg_spec(3), vec_spec, vec_spec, vec_spec,
                  pl.BlockSpec((3 * c, c), lambda hg, j: (0, 0)),
                  pl.BlockSpec((c, c), lambda hg, j: (0, 0))],
        out_specs=pl.BlockSpec((rows, bw), lambda hg, j: (chunk(j), hg)),
        scratch_shapes=[pltpu.VMEM((n_heads, HG_HEAD_DIM, HG_HEAD_DIM), F32)],
        compiler_params=pltpu.CompilerParams(
            dimension_semantics=("arbitrary", "arbitrary"),
            vmem_limit_bytes=_vmem_limit(24 * rows * bw * 4 + n_heads * HG_HEAD_DIM * HG_HEAD_DIM * 4)),
        name="hgrn2_scan_bw" if rev else "hgrn2_scan_fw",
    )(p_hg, p_hg, p_hg, loglb, l1mlb, omlb, mcat, tri)


def _shift_rows(u, down):
    n = u.shape[0]
    row = lax.broadcasted_iota(I32, u.shape, 0)
    if down:
        return jnp.where(row == 0, 0.0, pltpu.roll(u, 1, 0))
    return jnp.where(row == n - 1, 0.0, pltpu.roll(u, n - 1, 0))


def _to_scan_order(src_ref, dst_ref, rows):
    for col in range(GRID_W):
        dst_ref[col * rows:(col + 1) * rows, :] = src_ref[pl.ds(col, rows, stride=GRID_W), :]


def _conv_x_kernel(u_ref, w_ref, b_ref, o_ref, y_ref, *, rows):
    gw = GRID_W
    w0, w1, w2, bias = w_ref[0:1, :], w_ref[1:2, :], w_ref[2:3, :], b_ref[...]

    def slab(r):
        start = r * gw if isinstance(r, int) else pl.multiple_of(r * gw, gw)
        return pl.ds(start, gw)

    def emit(r, up, dn):
        y = w0 * up + w1 * u_ref[slab(r), :] + w2 * dn + bias
        y_ref[slab(r), :] = _silu(y)

    emit(0, _shift_rows(u_ref[slab(rows - 1), :], True), u_ref[slab(1), :])
    emit(rows - 1, u_ref[slab(rows - 2), :], _shift_rows(u_ref[slab(0), :], False))

    def body(r, carry):
        emit(r, u_ref[slab(r - 1), :], u_ref[slab(r + 1), :])
        return carry

    lax.fori_loop(1, rows - 1, body, 0)
    _to_scan_order(y_ref, o_ref, rows)


def _conv_c_kernel(u_ref, w_ref, b_ref, o_ref):
    u = u_ref[...]
    y = w_ref[0:1, :] * _shift_rows(u, True) + w_ref[1:2, :] * u + w_ref[2:3, :] * _shift_rows(u, False)
    o_ref[...] = _silu(y + b_ref[...])


def _permute_kernel(u_ref, o_ref, *, rows):
    _to_scan_order(u_ref, o_ref, rows)


def _unpermute_kernel(yf_ref, yb_ref, xs_ref, d_ref, o_ref, *, rows):
    d = d_ref[...]
    for col in range(GRID_W):
        sl = slice(col * rows, (col + 1) * rows)
        o_ref[pl.ds(col, rows, stride=GRID_W), :] = yf_ref[sl, :] + yb_ref[sl, :] + d * xs_ref[sl, :]


def _ssd_conv(p_xbc, conv_w, conv_b, n_x):
    t_all, f = p_xbc.shape
    rows = n_x // GRID_W
    fb = LANES
    blk = n_x * fb * 4
    xbc_x = pl.pallas_call(
        functools.partial(_conv_x_kernel, rows=rows),
        out_shape=jax.ShapeDtypeStruct((n_x, f), F32),
        grid=(f // fb,),
        in_specs=[pl.BlockSpec((n_x, fb), lambda k: (0, k)),
                  pl.BlockSpec((3, fb), lambda k: (0, k)),
                  pl.BlockSpec((1, fb), lambda k: (0, k))],
        out_specs=pl.BlockSpec((n_x, fb), lambda k: (0, k)),
        scratch_shapes=[pltpu.VMEM((n_x, fb), F32)],
        compiler_params=pltpu.CompilerParams(
            dimension_semantics=("arbitrary",), vmem_limit_bytes=_vmem_limit(5 * blk)),
        name="ssd_conv_latent",
    )(p_xbc, conv_w, conv_b)
    n_c = t_all - n_x
    fc = 1024
    xbc_c = pl.pallas_call(
        _conv_c_kernel,
        out_shape=jax.ShapeDtypeStruct((n_c, f), F32),
        grid=(f // fc,),
        in_specs=[pl.BlockSpec((n_c, fc), lambda k: (n_x // n_c, k)),
                  pl.BlockSpec((3, fc), lambda k: (0, k)),
                  pl.BlockSpec((1, fc), lambda k: (0, k))],
        out_specs=pl.BlockSpec((n_c, fc), lambda k: (0, k)),
        compiler_params=pltpu.CompilerParams(
            dimension_semantics=("arbitrary",), vmem_limit_bytes=_vmem_limit(4 * n_c * fc * 4)),
        name="ssd_conv_context",
    )(p_xbc, conv_w, conv_b)
    return xbc_x, xbc_c


def _permute_latent(u, n_x):
    rows = n_x // GRID_W
    return pl.pallas_call(
        functools.partial(_permute_kernel, rows=rows),
        out_shape=jax.ShapeDtypeStruct((n_x, u.shape[1]), F32),
        grid=(1,),
        in_specs=[pl.BlockSpec((n_x, u.shape[1]), lambda k: (0, 0))],
        out_specs=pl.BlockSpec((n_x, u.shape[1]), lambda k: (0, 0)),
        compiler_params=pltpu.CompilerParams(
            dimension_semantics=("arbitrary",), vmem_limit_bytes=_vmem_limit(4 * n_x * u.shape[1] * 4)),
        name="ssd_dt_to_scan_order",
    )(u)


def _unpermute_latent(y_fw, y_bw, xbc_x, d_row):
    n_x, w = y_fw.shape
    rows = n_x // GRID_W
    fb = LANES
    return pl.pallas_call(
        functools.partial(_unpermute_kernel, rows=rows),
        out_shape=jax.ShapeDtypeStruct((n_x, w), F32),
        grid=(w // fb,),
        in_specs=[pl.BlockSpec((n_x, fb), lambda k: (0, k)),
                  pl.BlockSpec((n_x, fb), lambda k: (0, k)),
                  pl.BlockSpec((n_x, fb), lambda k: (0, k)),
                  pl.BlockSpec((1, fb), lambda k: (0, k))],
        out_specs=pl.BlockSpec((n_x, fb), lambda k: (0, k)),
        compiler_params=pltpu.CompilerParams(
            dimension_semantics=("arbitrary",), vmem_limit_bytes=_vmem_limit(8 * n_x * fb * 4)),
        name="ssd_to_natural_order",
    )(y_fw, y_bw, xbc_x, d_row)


def _ssd_consts(rev):
    t = np.arange(SSD_CHUNK)
    tri = (t[None, :] >= t[:, None]) if rev else (t[None, :] <= t[:, None])
    return jnp.asarray(tri.astype(np.float32), BF16), jnp.asarray(tri.astype(np.float32), F32)


def _ssd_expanders(direction, n_heads):
    hpg = n_heads // SSD_GROUPS
    length = SSD_CHUNK
    e_lane = np.zeros((LANES, SSD_GROUPS * hpg * SSD_HEAD_DIM), np.float32)
    e_wide = np.zeros((LANES, SSD_GROUPS * hpg * length), np.float32)
    e_row = np.zeros((SSD_GROUPS * SUBLANES, LANES), np.float32)
    for g in range(SSD_GROUPS):
        for hh in range(hpg):
            col = direction * n_heads + g * hpg + hh
            e_lane[col, (g * hpg + hh) * SSD_HEAD_DIM:(g * hpg + hh + 1) * SSD_HEAD_DIM] = 1.0
            e_wide[col, (g * hpg + hh) * length:(g * hpg + hh + 1) * length] = 1.0
            e_row[g * SUBLANES + hh, col] = 1.0
    return jnp.asarray(e_lane, BF16), jnp.asarray(e_wide, BF16), jnp.asarray(e_row, BF16)


def _ssd_kernel(xs_ref, b_ref, c_ref, dt_ref, dtb_ref, a_ref, elane_ref, ewide_ref, erow_ref,
                tri01_ref, trif_ref, s0_ref, y_ref, sfin_ref, sgt_ref, *, rev, hpg, ngs):
    @pl.when(pl.program_id(1) == 0)
    def _():
        sgt_ref[...] = s0_ref[...]

    length = xs_ref.shape[0]
    p = SSD_HEAD_DIM
    gw = hpg * p
    nst = SSD_STATE
    dt_all = jax.nn.softplus(dt_ref[...] + dtb_ref[...])
    da_all = dt_all * a_ref[...]
    cum = _dot3(tri01_ref[...], da_all)
    tot = cum[0:1, :] if rev else cum[length - 1:length, :]
    dte = jnp.exp(tot - cum)
    ec = jnp.exp(cum)
    etot = jnp.broadcast_to(jnp.exp(tot), (SUBLANES, LANES))
    stacked = jnp.concatenate([dt_all, dte, ec, etot], axis=0)
    spread = _dot3_rhs01(stacked, elane_ref[...])
    dtb, dteb, ecb = spread[0:length], spread[length:2 * length], spread[2 * length:3 * length]
    cdb = spread[3 * length:3 * length + 1]
    cum_col = _dot3_rhs01(cum, ewide_ref[...])
    cum_row = _dot3(erow_ref[...], cum, (((1,), (1,)), ((), ())))

    xdt = xs_ref[...] * dtb
    xdtb = xdt.astype(BF16)
    xw = (xdt * dteb).astype(BF16)
    bmb, cmb = b_ref[...].astype(BF16), c_ref[...].astype(BF16)
    causal = trif_ref[...] > 0.5
    for g in range(ngs):
        bg, cg = bmb[:, g * nst:(g + 1) * nst], cmb[:, g * nst:(g + 1) * nst]
        cb = lax.dot_general(cg, bg, (((1,), (1,)), ((), ())), preferred_element_type=F32)
        ys = []
        for hh in range(hpg):
            hidx = g * hpg + hh
            seg = cum_col[:, hidx * length:(hidx + 1) * length] - cum_row[g * SUBLANES + hh:g * SUBLANES + hh + 1, :]
            dec = jnp.exp(jnp.where(causal, seg, -jnp.inf))
            m = (cb * dec).astype(BF16)
            ys.append(jnp.dot(m, xdtb[:, hidx * p:(hidx + 1) * p], preferred_element_type=F32))
        gsl = slice(g * gw, (g + 1) * gw)
        sgt = sgt_ref[g]
        y_off = jnp.dot(cg, sgt.astype(BF16), preferred_element_type=F32) * ecb[:, gsl]
        y_ref[:, gsl] = jnp.concatenate(ys, axis=1) + y_off
        upd = lax.dot_general(bg, xw[:, gsl], (((0,), (0,)), ((), ())), preferred_element_type=F32)
        sgt_new = sgt * cdb[:, gsl] + upd
        sgt_ref[g] = sgt_new
        sfin_ref[g] = sgt_new


def _ssd_scan(xbc, dt, dt_bias, a_row, s0, n_heads, direction, stream_name):
    n = xbc.shape[0]
    w = n_heads * SSD_HEAD_DIM
    hpg = n_heads // SSD_GROUPS
    gw = hpg * SSD_HEAD_DIM
    length = SSD_CHUNK
    nch = n // length
    rev = direction == 1
    e_lane, e_wide, e_row = _ssd_expanders(direction, n_heads)
    tri01, trif = _ssd_consts(rev)
    ngs = SSD_GROUPS_PER_STEP
    nb0 = w // (ngs * SSD_STATE)
    nbg = SSD_GROUPS // ngs

    def ch(j):
        return (nch - 1 - j) if rev else j

    const = lambda shape: pl.BlockSpec(shape, lambda g, j: (0,) * len(shape))
    return pl.pallas_call(
        functools.partial(_ssd_kernel, rev=rev, hpg=hpg, ngs=ngs),
        out_shape=(jax.ShapeDtypeStruct((n, w), F32),
                   jax.ShapeDtypeStruct((SSD_GROUPS, SSD_STATE, gw), F32)),
        grid=(nbg, nch),
        in_specs=[pl.BlockSpec((length, ngs * gw), lambda g, j: (ch(j), g)),
                  pl.BlockSpec((length, ngs * SSD_STATE), lambda g, j: (ch(j), nb0 + g)),
                  pl.BlockSpec((length, ngs * SSD_STATE), lambda g, j: (ch(j), nb0 + nbg + g)),
                  pl.BlockSpec((length, LANES), lambda g, j: (ch(j), 0)),
                  const((1, LANES)), const((1, LANES)),
                  pl.BlockSpec((LANES, ngs * gw), lambda g, j: (0, g)),
                  pl.BlockSpec((LANES, ngs * hpg * length), lambda g, j: (0, g)),
                  pl.BlockSpec((ngs * SUBLANES, LANES), lambda g, j: (g, 0)),
                  const((length, length)), const((length, length)),
                  pl.BlockSpec((ngs, SSD_STATE, gw), lambda g, j: (g, 0, 0))],
        out_specs=(pl.BlockSpec((length, ngs * gw), lambda g, j: (ch(j), g)),
                   pl.BlockSpec((ngs, SSD_STATE, gw), lambda g, j: (g, 0, 0))),
        scratch_shapes=[pltpu.VMEM((ngs, SSD_STATE, gw), F32)],
        compiler_params=pltpu.CompilerParams(dimension_semantics=("arbitrary", "arbitrary"),
                                             vmem_limit_bytes=_vmem_limit(24 * 1024 * 1024)),
        name="ssd_scan_%s_%s" % (stream_name, "bw" if rev else "fw"),
    )(xbc, xbc, xbc, dt, dt_bias, a_row, e_lane, e_wide, e_row, tri01, trif, s0)


def _readout_kernel(of_ref, ob_ref, g_ref, hw_ref, yx_ref, ycf_ref, ycb_ref, xsc_ref, z_ref, d_ref, sw_ref,
                    o_ref, *, n_xblk, hg_w, ssd_group):
    i = pl.program_id(0)
    o = of_ref[...] + ob_ref[...]
    gate = _silu(g_ref[...])
    hw = hw_ref[...]
    for h in range(hg_w // HG_HEAD_DIM):
        sl = slice(h * HG_HEAD_DIM, (h + 1) * HG_HEAD_DIM)
        oh = o[:, sl]
        ms = jnp.mean(oh * oh, axis=-1, keepdims=True)
        o_ref[:, sl] = (oh * lax.rsqrt(ms + EPS) * hw[:, sl] * gate[:, sl]).astype(o_ref.dtype)

    y_c = ycf_ref[...] + ycb_ref[...] + d_ref[...] * xsc_ref[...]
    y = jnp.where(i < n_xblk, yx_ref[...], y_c) * _silu(z_ref[...])
    sw = sw_ref[...]
    for g in range(y.shape[1] // ssd_group):
        sl = slice(g * ssd_group, (g + 1) * ssd_group)
        yg = y[:, sl]
        ms = jnp.mean(yg * yg, axis=-1, keepdims=True)
        o_ref[:, hg_w + g * ssd_group:hg_w + (g + 1) * ssd_group] = (
            yg * lax.rsqrt(ms + EPS) * sw[:, sl]).astype(o_ref.dtype)


def _readout(o_fw, o_bw, p_hg, hg_norm_w, y_x, yc_fw, yc_bw, xbc_c, p_z, d_row, ssd_norm_w, n_x):
    t_all, hg_w = o_fw.shape
    sw_w = p_z.shape[1]
    n_c = t_all - n_x
    tm = n_c
    n_xblk = n_x // tm
    xi = lambda i: (jnp.minimum(i, n_xblk - 1), 0)
    ci = lambda i: (0, 0)
    need = 2 * tm * 4 * (3 * hg_w + 5 * sw_w) + 2 * tm * (hg_w + sw_w) * 2 + 4 * tm * (hg_w + sw_w) * 4
    return pl.pallas_call(
        functools.partial(_readout_kernel, n_xblk=n_xblk, hg_w=hg_w, ssd_group=sw_w // SSD_GROUPS),
        out_shape=jax.ShapeDtypeStruct((t_all, hg_w + sw_w), BF16),
        grid=(t_all // tm,),
        in_specs=[pl.BlockSpec((tm, hg_w), lambda i: (i, 0)),
                  pl.BlockSpec((tm, hg_w), lambda i: (i, 0)),
                  pl.BlockSpec((tm, hg_w), lambda i: (i, 4)),
                  pl.BlockSpec((1, hg_w), lambda i: (0, 0)),
                  pl.BlockSpec((tm, sw_w), xi),
                  pl.BlockSpec((tm, sw_w), ci), pl.BlockSpec((tm, sw_w), ci), pl.BlockSpec((tm, sw_w), ci),
                  pl.BlockSpec((tm, sw_w), lambda i: (i, 0)),
                  pl.BlockSpec((1, sw_w), lambda i: (0, 0)),
                  pl.BlockSpec((1, sw_w), lambda i: (0, 0))],
        out_specs=pl.BlockSpec((tm, hg_w + sw_w), lambda i: (i, 0)),
        compiler_params=pltpu.CompilerParams(dimension_semantics=("arbitrary",),
                                             vmem_limit_bytes=_vmem_limit(need)),
        name="mixer_readout",
    )(o_fw, o_bw, p_hg, hg_norm_w, y_x, yc_fw, yc_bw, xbc_c, p_z, d_row, ssd_norm_w)


def _norm_route_kernel(x_ref, nw_ref, sh_ref, sc_ref, w_ref, b_ref, pk_ref, id_ref, wt_ref, *, n_x, tm):
    i = pl.program_id(0)
    x = x_ref[...]
    ms = jnp.mean(x * x, axis=-1, keepdims=True)
    f = x * lax.rsqrt(ms + EPS) * nw_ref[...]
    f = f * (1.0 + _row_select(i, tm, n_x, sc_ref)) + _row_select(i, tm, n_x, sh_ref)
    pk_ref[...] = _pack_bf16_pairs(f)

    logits = jnp.dot(f, w_ref[...], preferred_element_type=F32,
                     precision=lax.Precision.HIGHEST) + b_ref[...]
    lane_i = lax.broadcasted_iota(I32, (tm, LANES), 1)
    lane = lane_i.astype(F32)
    neg = -jnp.inf
    big = float(LANES)
    is_g = lane_i < MOE_GROUPS
    gl = jnp.where(is_g, logits, neg)
    gmax = jnp.max(gl, axis=-1, keepdims=True)
    gsum = jnp.sum(jnp.where(is_g, jnp.exp(gl - gmax), 0.0), axis=-1, keepdims=True)
    p_top = 1.0 / gsum
    g_idx = jnp.min(jnp.where(gl == gmax, lane, big), axis=-1, keepdims=True)
    e_lo = MOE_GROUPS + g_idx * EXPERTS_PER_GROUP
    in_grp = (lane >= e_lo) & (lane < e_lo + EXPERTS_PER_GROUP)
    el = jnp.where(in_grp, logits, neg)
    m1 = jnp.max(el, axis=-1, keepdims=True)
    i1 = jnp.min(jnp.where(in_grp & (el == m1), lane, big), axis=-1, keepdims=True)
    el2 = jnp.where(lane == i1, neg, el)
    m2 = jnp.max(el2, axis=-1, keepdims=True)
    i2 = jnp.min(jnp.where(in_grp & (lane != i1) & (el2 == m2), lane, big), axis=-1, keepdims=True)
    e2 = jnp.exp(m2 - m1)
    w1 = p_top / (1.0 + e2)
    w2 = p_top * e2 / (1.0 + e2)
    ids = jnp.where(lane_i == 0, i1 - MOE_GROUPS, jnp.where(lane_i == 1, i2 - MOE_GROUPS, 0.0))
    id_ref[...] = ids.astype(I32)
    wt_ref[...] = jnp.where(lane_i == 0, w1, jnp.where(lane_i == 1, w2, 0.0))


def _norm_route(x, norm_w, shift2, scale2, w_r, b_r, n_rows, n_x):
    d = x.shape[1]
    tm = 256
    row = lambda w: pl.BlockSpec((tm, w), lambda i: (i, 0))
    fixed = lambda r, w: pl.BlockSpec((r, w), lambda i: (0, 0))
    return pl.pallas_call(
        functools.partial(_norm_route_kernel, n_x=n_x, tm=tm),
        out_shape=(jax.ShapeDtypeStruct((n_rows, d // 2), jnp.uint32),
                   jax.ShapeDtypeStruct((n_rows, LANES), I32), jax.ShapeDtypeStruct((n_rows, LANES), F32)),
        grid=(n_rows // tm,),
        in_specs=[row(d), fixed(1, d), fixed(2, d), fixed(2, d), fixed(d, LANES), fixed(1, LANES)],
        out_specs=(row(d // 2), row(LANES), row(LANES)),
        compiler_params=pltpu.CompilerParams(dimension_semantics=("arbitrary",),
                                             vmem_limit_bytes=_vmem_limit(8 * tm * d * 4 + 4 * d * LANES * 4)),
        name="moe_norm_route",
    )(x, norm_w, shift2, scale2, w_r, b_r)


GATHER_UNROLL = 8


def _expert_kernel(tok_ref, blk_e_ref, nvalid_ref, first_ref, next_e_ref, n_used_ref,
                   src_ref, wg_hbm, wu_hbm, wd_hbm, o_ref,
                   xbuf_ref, wg_st, wu_st, wd_st, wg_bf, wu_bf, wd_bf, gsem_ref, wsem_ref, *, rows, layer):
    b = pl.program_id(0)
    n_used = n_used_ref[0]
    slot = lax.rem(b, 2)

    def row_copy(blk, slt, r):
        return pltpu.make_async_copy(src_ref.at[pl.ds(tok_ref[blk * rows + r], 1), :],
                                     xbuf_ref.at[slt, pl.ds(r, 1), :], gsem_ref.at[slt])

    def for_valid_rows(blk, fn):
        def body(g, carry):
            for u in range(GATHER_UNROLL):
                fn(g * GATHER_UNROLL + u)
            return carry
        lax.fori_loop(0, (nvalid_ref[blk] + GATHER_UNROLL - 1) // GATHER_UNROLL, body, 0)

    def start_block(blk, slt):
        for_valid_rows(blk, lambda r: row_copy(blk, slt, r).start())

    def wait_block(blk, slt):
        for_valid_rows(blk, lambda r: row_copy(blk, slt, r).wait())

    def weight_copies(e):
        return (pltpu.make_async_copy(wg_hbm.at[layer, e], wg_st, wsem_ref.at[0]),
                pltpu.make_async_copy(wu_hbm.at[layer, e], wu_st, wsem_ref.at[1]),
                pltpu.make_async_copy(wd_hbm.at[layer, e], wd_st, wsem_ref.at[2]))

    def start_weights(e):
        for cp in weight_copies(e):
            cp.start(priority=1)

    @pl.when((b == 0) & (n_used > 0))
    def _():
        xbuf_ref[...] = jnp.zeros_like(xbuf_ref)
        start_block(0, 0)
        start_weights(blk_e_ref[0])

    @pl.when(b + 1 < n_used)
    def _():
        start_block(b + 1, 1 - slot)

    @pl.when((b < n_used) & (first_ref[b] == 1))
    def _():
        for cp in weight_copies(blk_e_ref[b]):
            cp.wait()
        wg_bf[...] = wg_st[...].astype(BF16)
        wu_bf[...] = wu_st[...].astype(BF16)
        wd_bf[...] = wd_st[...].astype(BF16)

        @pl.when(next_e_ref[b] >= 0)
        def _():
            start_weights(next_e_ref[b])

    @pl.when(b < n_used)
    def _():
        wait_block(b, slot)
        xa, xb = _unpack_bf16_pairs(xbuf_ref[slot])
        half = xa.shape[1]
        xa, xb = xa.astype(BF16), xb.astype(BF16)

        def proj(w_ref):
            return (jnp.dot(xa, w_ref[0:half, :], preferred_element_type=F32)
                    + jnp.dot(xb, w_ref[half:2 * half, :], preferred_element_type=F32))

        h = (_silu(proj(wg_bf)) * proj(wu_bf)).astype(BF16)
        y = jnp.dot(h, wd_bf[...], preferred_element_type=F32)
        o_ref[...] = _pack_bf16_pairs(y)

    @pl.when(b >= n_used)
    def _():
        o_ref[...] = jnp.zeros_like(o_ref)


def _expert_mlp(tables, packed, wg_all, wu_all, wd_all, layer):
    buf_tok, blk_e, nvalid, first, next_e, n_used = tables
    d, hid = wg_all.shape[2], wg_all.shape[3]
    rows = MOE_ROWS
    nr = buf_tok.shape[0]
    hbm = pl.BlockSpec(memory_space=pl.ANY)
    need = 3 * d * hid * (4 + 2) + 2 * rows * (d // 2) * 4 + 2 * rows * (d // 2) * 4 + rows * d * 8 + 6 * rows * hid * 4
    return pl.pallas_call(
        functools.partial(_expert_kernel, rows=rows, layer=layer),
        out_shape=jax.ShapeDtypeStruct((nr, d // 2), jnp.uint32),
        grid_spec=pltpu.PrefetchScalarGridSpec(
            num_scalar_prefetch=6, grid=(nr // rows,),
            in_specs=[hbm, hbm, hbm, hbm],
            out_specs=pl.BlockSpec((rows, d // 2), lambda b, *_: (b, 0)),
            scratch_shapes=[pltpu.VMEM((2, rows, d // 2), jnp.uint32),
                            pltpu.VMEM((d, hid), F32), pltpu.VMEM((d, hid), F32), pltpu.VMEM((hid, d), F32),
                            pltpu.VMEM((d, hid), BF16), pltpu.VMEM((d, hid), BF16), pltpu.VMEM((hid, d), BF16),
                            pltpu.SemaphoreType.DMA((2,)), pltpu.SemaphoreType.DMA((3,))]),
        compiler_params=pltpu.CompilerParams(dimension_semantics=("arbitrary",),
                                             vmem_limit_bytes=_vmem_limit(need)),
        name="moe_expert_mlp",
    )(buf_tok, blk_e, nvalid, first, next_e, n_used, packed, wg_all, wu_all, wd_all)


def _combine_kernel(d0_ref, d1_ref, yb_ref, x_ref, wt_ref, g_ref, nw_ref, o_ref, buf_ref, sem_ref,
                    *, rows, n_x, final_norm):
    i = pl.program_id(0)
    base = i * rows
    slot = lax.rem(i, 2)

    def copies(blk, slt, r):
        return (pltpu.make_async_copy(yb_ref.at[pl.ds(d0_ref[blk * rows + r], 1), :],
                                      buf_ref.at[slt, 0, pl.ds(r, 1), :], sem_ref.at[slt, 0]),
                pltpu.make_async_copy(yb_ref.at[pl.ds(d1_ref[blk * rows + r], 1), :],
                                      buf_ref.at[slt, 1, pl.ds(r, 1), :], sem_ref.at[slt, 1]))

    def start_block(blk, slt):
        def body(r, carry):
            for k, cp in enumerate(copies(blk, slt, r)):
                cp.start(priority=k)
            return carry
        lax.fori_loop(0, rows, body, 0, unroll=8)

    def wait_block(blk, slt):
        def body(r, carry):
            for cp in copies(blk, slt, r):
                cp.wait()
            return carry
        lax.fori_loop(0, rows, body, 0, unroll=8)

    @pl.when(i == 0)
    def _():
        start_block(0, 0)

    @pl.when(i + 1 < pl.num_programs(0))
    def _():
        start_block(i + 1, 1 - slot)

    wait_block(i, slot)
    half = buf_ref.shape[3]
    col_halves = (slice(0, half), slice(half, 2 * half))
    ch = COMBINE_ROWS_PER_TRIP

    def finish(rc, carry):
        rsl = pl.ds(pl.multiple_of(rc * ch, ch), ch)
        wt = wt_ref[rsl, :]
        w0, w1 = wt[:, 0:1], wt[:, 1:2]
        row = base + rc * ch + lax.broadcasted_iota(I32, (ch, 1), 0)
        gate = jnp.where(row < n_x, g_ref[0:1, :], g_ref[1:2, :])
        halves = []
        for part0, part1, sl in zip(_unpack_bf16_pairs(buf_ref[slot, 0, rsl, :]),
                                    _unpack_bf16_pairs(buf_ref[slot, 1, rsl, :]), col_halves):
            halves.append(x_ref[rsl, sl] + gate[:, sl] * (part0 * w0 + part1 * w1))
        if final_norm:
            ssq = sum(jnp.sum(hv * hv, axis=-1, keepdims=True) for hv in halves)
            scale = lax.rsqrt(ssq / (2 * half) + EPS)
            halves = [hv * scale * nw_ref[:, sl] for hv, sl in zip(halves, col_halves)]
        for hv, sl in zip(halves, col_halves):
            o_ref[rsl, sl] = hv
        return carry

    lax.fori_loop(0, rows // ch, finish, 0)


def _combine(dest0, dest1, yb, x_stream, wts, gate2, norm_w, n_rows, n_x, final_norm):
    d = 2 * yb.shape[1]
    rows = MOE_ROWS
    return pl.pallas_call(
        functools.partial(_combine_kernel, rows=rows, n_x=n_x, final_norm=final_norm),
        out_shape=jax.ShapeDtypeStruct((n_rows, d), F32),
        grid_spec=pltpu.PrefetchScalarGridSpec(
            num_scalar_prefetch=2, grid=(n_rows // rows,),
            in_specs=[pl.BlockSpec(memory_space=pl.ANY),
                      pl.BlockSpec((rows, d), lambda i, a, b: (i, 0)),
                      pl.BlockSpec((rows, LANES), lambda i, a, b: (i, 0)),
                      pl.BlockSpec((2, d), lambda i, a, b: (0, 0)),
                      pl.BlockSpec((1, d), lambda i, a, b: (0, 0))],
            out_specs=pl.BlockSpec((rows, d), lambda i, a, b: (i, 0)),
            scratch_shapes=[pltpu.VMEM((2, 2, rows, d // 2), jnp.uint32), pltpu.SemaphoreType.DMA((2, 2))]),
        compiler_params=pltpu.CompilerParams(dimension_semantics=("arbitrary",),
                                             vmem_limit_bytes=_vmem_limit(6 * rows * d * 4 + 4 * rows * d * 4)),
        name="moe_combine",
    )(dest0, dest1, yb, x_stream, wts, gate2, norm_w)


def _dispatch_tables(eid, n_tok):
    n_assign = n_tok * TOP_K
    flat_e = eid.reshape(-1)
    order = jnp.argsort(flat_e, stable=True)
    se = flat_e[order]
    counts = jnp.bincount(flat_e, length=N_EXPERTS)
    padded = (counts + MOE_ROWS - 1) // MOE_ROWS * MOE_ROWS
    pad_end = jnp.cumsum(padded)
    pad_start = pad_end - padded
    start = jnp.cumsum(counts) - counts
    dest_sorted = (pad_start[se] + jnp.arange(n_assign, dtype=I32) - start[se]).astype(I32)
    dest = dest_sorted[jnp.argsort(order)].reshape(n_tok, TOP_K)
    n_blocks = -(-n_assign // MOE_ROWS) + N_EXPERTS
    blk_start = jnp.arange(n_blocks, dtype=I32) * MOE_ROWS
    blk_e = jnp.minimum(jnp.searchsorted(pad_end, blk_start, side='right'), N_EXPERTS - 1).astype(I32)
    n_used = (pad_end[-1] // MOE_ROWS).astype(I32)
    nvalid = jnp.clip(counts[blk_e] - (blk_start - pad_start[blk_e]), 0, MOE_ROWS).astype(I32)
    row_e = jnp.repeat(blk_e, MOE_ROWS)
    rank = jnp.arange(n_blocks * MOE_ROWS, dtype=I32) - pad_start[row_e]
    src = jnp.clip(start[row_e] + rank, 0, n_assign - 1)
    buf_tok = jnp.where(rank < counts[row_e], order[src] // TOP_K, 0).astype(I32)
    first = jnp.concatenate([jnp.ones((1,), I32), (blk_e[1:] != blk_e[:-1]).astype(I32)])
    nxt = jnp.searchsorted(blk_e, blk_e, side='right').astype(I32)
    next_e = jnp.where(nxt < n_used, blk_e[jnp.minimum(nxt, n_blocks - 1)], -1).astype(I32)
    return (buf_tok, blk_e, nvalid, first, next_e, n_used.reshape(1)), dest


def _hier_moe(stream, n_rows, n_x, norm_w, shift2, scale2, gate2, w_r, b_r, wg_all, wu_all, wd_all, layer, final_w):
    packed, ids, wts = _norm_route(stream, norm_w, shift2, scale2, w_r, b_r, n_rows, n_x)
    tables, dest = _dispatch_tables(ids[:, :TOP_K], n_rows)
    yb = _expert_mlp(tables, packed, wg_all, wu_all, wd_all, layer)
    final_norm = final_w is not None
    nw = final_w if final_norm else norm_w
    return _combine(dest[:, 0], dest[:, 1], yb, stream, wts, gate2, nw, n_rows, n_x, final_norm)


def kernel(x, c, ctx, c_ctx, w_mod, b_mod, norm1_w, w_in, hgrn_lower_bounds, hgrn_norm_w, ssd_conv_w, ssd_conv_b,
           ssd_dt_bias, ssd_a_log, ssd_d, ssd_norm_w, w_out, norm2_w, w_group_router, b_group_router,
           w_expert_router, b_expert_router, w_gate, w_up, w_down, final_norm_w):
    bsz, seq, d = x.shape
    assert bsz == 1
    n_ctx = ctx.shape[1]
    depth = w_mod.shape[0]
    t_all = seq + n_ctx
    hg_w = d // 2
    ssd_w = d - hg_w
    n_ssd_heads = ssd_w // SSD_HEAD_DIM
    hg_cols = 5 * hg_w
    conv_dim = ssd_w + 2 * SSD_GROUPS * SSD_STATE
    assert seq // GRID_W == SSD_CHUNK and n_ctx % SSD_CHUNK == 0 and seq % n_ctx == 0
    gw = n_ssd_heads // SSD_GROUPS * SSD_HEAD_DIM
    dt_pad = LANES - 2 * n_ssd_heads

    stream = jnp.concatenate([x[0], ctx[0]], axis=0)
    c2t = jnp.stack([c[0], c_ctx], axis=1)
    lb_soft = jax.nn.softmax(hgrn_lower_bounds.astype(F32), axis=0)
    lower_bounds = jnp.cumsum(lb_soft, axis=0) - lb_soft[0]
    s_zero = jnp.zeros((SSD_GROUPS, SSD_STATE, gw), F32)
    w_in_t = jnp.swapaxes(w_in, 1, 2)

    for l in range(depth):
        last = l == depth - 1
        mod = _mod_vectors(c2t, w_mod, l, b_mod[l][None, :]).reshape(2, MOD_CHUNKS, d)
        chunk = lambda k: mod[:, k, :]

        h = _norm_mod(stream, norm1_w[l][None, :], chunk(0), chunk(1), t_all, seq, BF16)
        p_hg = _matmul_nt(h, w_in_t, l, 0, hg_cols)
        p_z = _matmul_nt(h, w_in_t, l, hg_cols, ssd_w)
        p_xbc = _matmul_nt(h, w_in_t, l, hg_cols + ssd_w, conv_dim)
        w_dt = jnp.pad(w_in_t[l, hg_cols + ssd_w + conv_dim:, :], ((0, dt_pad), (0, 0)))
        p_dt = _matmul_nt(h, w_dt[None], 0, 0, LANES)

        lb = lower_bounds[l]
        outs = []
        for direction in range(2):
            lbd = lb[direction][None, :]
            outs.append(_hgrn_scan(p_hg, 1 + direction, jnp.log(lbd), jnp.log1p(-lbd), 1.0 - lbd, seq,
                                   rev=direction == 1))
        o_fw, o_bw = outs

        xbc_x, xbc_c = _ssd_conv(p_xbc, ssd_conv_w[l], ssd_conv_b[l][None, :], seq)
        dt_bias = jnp.pad(ssd_dt_bias[l].reshape(1, -1), ((0, 0), (0, dt_pad)))
        a_row = jnp.pad(-jnp.exp(ssd_a_log[l].astype(F32)).reshape(1, -1), ((0, 0), (0, dt_pad)))
        dt_x = _permute_latent(p_dt, seq)
        dt_c = p_dt[seq:]
        ys_x, ys_c = [], []
        for direction in range(2):
            y_c, s_c = _ssd_scan(xbc_c, dt_c, dt_bias, a_row, s_zero, n_ssd_heads, direction, "context")
            y_x, _ = _ssd_scan(xbc_x, dt_x, dt_bias, a_row, s_c, n_ssd_heads, direction, "latent")
            ys_c.append(y_c)
            ys_x.append(y_x)
        d_row = jnp.repeat(ssd_d[l].astype(F32), SSD_HEAD_DIM)[None, :]
        y_x = _unpermute_latent(ys_x[0], ys_x[1], xbc_x, d_row)
        ab = _readout(o_fw, o_bw, p_hg, jnp.tile(hgrn_norm_w[l], hg_w // HG_HEAD_DIM)[None, :],
                      y_x, ys_c[0], ys_c[1], xbc_c, p_z, d_row, ssd_norm_w[l][None, :], seq)
        stream = _matmul_residual(ab, w_out, l, stream, chunk(2), seq)

        w_r = jnp.pad(jnp.concatenate([w_group_router[l], w_expert_router[l]], axis=1),
                      ((0, 0), (0, LANES - MOE_GROUPS - N_EXPERTS)))
        b_r = jnp.pad(jnp.concatenate([b_group_router[l], b_expert_router[l]]),
                      (0, LANES - MOE_GROUPS - N_EXPERTS))[None, :]
        n_rows = seq if last else t_all
        stream = _hier_moe(stream, n_rows, seq, norm2_w[l][None, :], chunk(3), chunk(4), chunk(5), w_r, b_r,
                           w_gate, w_up, w_down, l, final_norm_w[None, :] if last else None)
    return stream[:seq].reshape(bsz, seq, d)
```
